```python
import jax, jax.numpy as jnp
from jax import lax
import numpy as np

D_MODEL = 1024
BATCH = 8
SEQ = 4096
DEPTH = 1

D_RNN = 1536
RNN_BLOCKS = 16
RNN_BLOCK_W = D_RNN // RNN_BLOCKS
RNN_CONV_W = 4
RNN_CONV_PAD = (2, 1)
RG_C = 8.0
D_CONV = D_MODEL
CONV_W = 31
CONV_PAD = ((CONV_W - 1) // 2, (CONV_W - 1) // 2)
D_FF = 4 * D_MODEL
N_BRANCH = 2
LN_EPS = 1e-5
DEEPNORM_ALPHA = (2.0 * DEPTH) ** 0.25
DEEPNORM_BETA = (8.0 * DEPTH) ** -0.25
SPLIT_RNN_X = D_RNN
SPLIT_RNN_Y = SPLIT_RNN_X + D_RNN
SPLIT_CONV = SPLIT_RNN_Y + 2 * D_CONV
D_IN = SPLIT_CONV + N_BRANCH * D_MODEL

kernel_name = "hybrid_rglru_conformer_encoder_layer"


def layer_norm(x, g, b):
    xf = x.astype(jnp.float32)
    mu = jnp.mean(xf, axis=-1, keepdims=True)
    var = jnp.mean(jnp.square(xf - mu), axis=-1, keepdims=True)
    return ((xf - mu) * lax.rsqrt(var + LN_EPS)).astype(x.dtype) * g + b


def depthwise_conv(x, w, b, pad):
    y = lax.conv_general_dilated(
        x, w[:, None, :], window_strides=(1,), padding=[pad],
        dimension_numbers=("NWC", "WIO", "NWC"), feature_group_count=x.shape[-1])
    return y + b


def rg_lru_bidir(u, gate_w, gate_b, a_param):
    B, T, _ = u.shape
    ub = u.reshape(B, T, RNN_BLOCKS, RNN_BLOCK_W)
    gates = jnp.einsum("btni,dgnio->dgbtno", ub, gate_w) + gate_b[:, :, None, None]
    gates = jax.nn.sigmoid(gates.astype(jnp.float32)).reshape(2, 2, B, T, D_RNN)
    r, i = gates[:, 0], gates[:, 1]
    log_a = -RG_C * r * jax.nn.softplus(-a_param.astype(jnp.float32))[:, None, None, :]
    a = jnp.exp(log_a)
    bx = jnp.sqrt(-jnp.expm1(2.0 * log_a)) * (i * u.astype(jnp.float32)[None])
    a = jnp.stack([a[0], jnp.flip(a[1], axis=1)])
    bx = jnp.stack([bx[0], jnp.flip(bx[1], axis=1)])
    a_t = jnp.moveaxis(a, 2, 0)
    b_t = jnp.moveaxis(bx, 2, 0)

    def step(h, ab):
        at, bt = ab
        h = at * h + bt
        return h, h

    _, hs = lax.scan(step, jnp.zeros((2, B, D_RNN), jnp.float32), (a_t, b_t))
    hs = jnp.moveaxis(hs, 0, 2)
    return (hs[0] + jnp.flip(hs[1], axis=1)).astype(u.dtype)


def setup_inputs(seed: int = 0) -> dict:
    key = jax.random.key(seed)
    ks = jax.random.split(key, 32)
    f32 = jnp.float32
    L = DEPTH

    def nrm(k, shape, scale):
        return jax.random.normal(k, shape, f32) * scale

    x = jax.random.normal(ks[0], (BATCH, SEQ, D_MODEL), f32)
    emb_ln_g = 1.0 + nrm(ks[1], (D_MODEL,), 0.02)
    emb_ln_b = nrm(ks[2], (D_MODEL,), 0.02)
    w_in = nrm(ks[3], (L, D_MODEL, D_IN), D_MODEL ** -0.5)
    b_in = nrm(ks[4], (L, D_IN), 0.02)
    rnn_conv_w = nrm(ks[5], (L, RNN_CONV_W, D_RNN), RNN_CONV_W ** -0.5)
    rnn_conv_b = nrm(ks[6], (L, D_RNN), 0.02)
    rg_gate_w = nrm(ks[7], (L, 2, 2, RNN_BLOCKS, RNN_BLOCK_W, RNN_BLOCK_W), RNN_BLOCK_W ** -0.5)
    rg_gate_b = nrm(ks[8], (L, 2, 2, RNN_BLOCKS, RNN_BLOCK_W), 0.02)
    a_pow = jax.random.uniform(ks[9], (L, 2, D_RNN), f32, 0.9, 0.999)
    s = a_pow ** (1.0 / RG_C)
    rg_a_param = jnp.log(s) - jnp.log1p(-s)
    w_branch_a = nrm(ks[10], (L, D_RNN, D_MODEL), D_RNN ** -0.5 * DEEPNORM_BETA)
    conv_w = nrm(ks[11], (L, CONV_W, D_CONV), CONV_W ** -0.5)
    conv_b = nrm(ks[12], (L, D_CONV), 0.02)
    conv_ln_g = 1.0 + nrm(ks[13], (L, D_CONV), 0.02)
    conv_ln_b = nrm(ks[14], (L, D_CONV), 0.02)
    w_branch_b = nrm(ks[15], (L, D_CONV, D_MODEL), D_CONV ** -0.5 * DEEPNORM_BETA)
    w_out = nrm(ks[16], (L, D_MODEL, D_MODEL), D_MODEL ** -0.5 * DEEPNORM_BETA)
    b_out = nrm(ks[17], (L, D_MODEL), 0.02)
    ln1_g = 1.0 + nrm(ks[18], (L, D_MODEL), 0.02)
    ln1_b = nrm(ks[19], (L, D_MODEL), 0.02)
    w_up = nrm(ks[20], (L, D_MODEL, D_FF), D_MODEL ** -0.5 * DEEPNORM_BETA)
    b_up = nrm(ks[21], (L, D_FF), 0.02)
    w_down = nrm(ks[22], (L, D_FF, D_MODEL), D_FF ** -0.5 * DEEPNORM_BETA)
    b_down = nrm(ks[23], (L, D_MODEL), 0.02)
    ln2_g = 1.0 + nrm(ks[24], (L, D_MODEL), 0.02)
    ln2_b = nrm(ks[25], (L, D_MODEL), 0.02)
    return {
        "x": x, "emb_ln_g": emb_ln_g, "emb_ln_b": emb_ln_b,
        "w_in": w_in, "b_in": b_in,
        "rnn_conv_w": rnn_conv_w, "rnn_conv_b": rnn_conv_b,
        "rg_gate_w": rg_gate_w, "rg_gate_b": rg_gate_b, "rg_a_param": rg_a_param,
        "w_branch_a": w_branch_a,
        "conv_w": conv_w, "conv_b": conv_b, "conv_ln_g": conv_ln_g, "conv_ln_b": conv_ln_b,
        "w_branch_b": w_branch_b,
        "w_out": w_out, "b_out": b_out, "ln1_g": ln1_g, "ln1_b": ln1_b,
        "w_up": w_up, "b_up": b_up, "w_down": w_down, "b_down": b_down,
        "ln2_g": ln2_g, "ln2_b": ln2_b,
    }


def reference(x, emb_ln_g, emb_ln_b, w_in, b_in, rnn_conv_w, rnn_conv_b, rg_gate_w, rg_gate_b,
              rg_a_param, w_branch_a, conv_w, conv_b, conv_ln_g, conv_ln_b, w_branch_b,
              w_out, b_out, ln1_g, ln1_b, w_up, b_up, w_down, b_down, ln2_g, ln2_b):
    B, T, _ = x.shape
    h = layer_norm(x, emb_ln_g, emb_ln_b)
    for l in range(DEPTH):
        proj = h @ w_in[l] + b_in[l]
        xr = proj[..., :SPLIT_RNN_X]
        yr = proj[..., SPLIT_RNN_X:SPLIT_RNN_Y]
        xc = proj[..., SPLIT_RNN_Y:SPLIT_CONV]
        gl = proj[..., SPLIT_CONV:]

        xr = depthwise_conv(xr, rnn_conv_w[l], rnn_conv_b[l], RNN_CONV_PAD)
        hr = rg_lru_bidir(xr, rg_gate_w[l], rg_gate_b[l], rg_a_param[l])
        y_a = (jax.nn.gelu(yr) * hr) @ w_branch_a[l]

        c = jax.nn.glu(xc, axis=-1)
        c = depthwise_conv(c, conv_w[l], conv_b[l], CONV_PAD)
        c = jax.nn.silu(layer_norm(c, conv_ln_g[l], conv_ln_b[l]))
        y_b = c @ w_branch_b[l]

        g = jax.nn.sigmoid(gl).reshape(B, T, N_BRANCH, D_MODEL)
        mixed = (g[:, :, 0] * y_a + g[:, :, 1] * y_b) @ w_out[l] + b_out[l]
        h = layer_norm(DEEPNORM_ALPHA * h + mixed, ln1_g[l], ln1_b[l])

        m = jnp.square(jax.nn.relu(h @ w_up[l] + b_up[l]))
        h = layer_norm(DEEPNORM_ALPHA * h + (m @ w_down[l] + b_down[l]), ln2_g[l], ln2_b[l])
    return h
```

```python
import functools

import jax
import jax.numpy as jnp
from jax.experimental import pallas as pl
from jax.experimental.pallas import tpu as pltpu

D_MODEL = 1024
D_RNN = 1536
RNN_BLOCKS = 16
RNN_BLOCK_W = D_RNN // RNN_BLOCKS
GATE_GROUP = 768
RNN_CONV_W = 4
RNN_CONV_LEFT = 2
RG_C = 8.0
CONV_W = 31
CONV_HALF = (CONV_W - 1) // 2
D_FF = 4 * D_MODEL
LN_EPS = 1e-5
DEEPNORM_ALPHA = 2.0 ** 0.25

SUBLANES = 8
ROWS = 512
SUB_ROWS = 256
CONV_CHUNK = 32
HALO_ROWS = 128
VMEM_LIMIT = 56 * 1024 * 1024

_BF16 = jnp.bfloat16
_F32 = jnp.float32


def _ln(x, g, b):
    mu = jnp.mean(x, axis=-1, keepdims=True)
    xc = x - mu
    var = jnp.mean(xc * xc, axis=-1, keepdims=True)
    return xc * jax.lax.rsqrt(var + LN_EPS) * g + b


def _sigmoid(x):
    return 0.5 * jnp.tanh(0.5 * x) + 0.5


def _gelu_tanh(x):
    return 0.5 * x * (1.0 + jnp.tanh(0.7978845608028654 * (x + 0.044715 * (x * x * x))))


def _dot(a, w):
    return jnp.dot(a.astype(_BF16), w, preferred_element_type=_F32)


def _gate_terms(u, wg_ref, bg_ref, decay_ref):
    ub = u.astype(_BF16)
    r_parts, i_parts = [], []
    for q in range(D_RNN // GATE_GROUP):
        g = jnp.dot(ub[:, q * GATE_GROUP:(q + 1) * GATE_GROUP], wg_ref[q],
                    preferred_element_type=_F32)
        r_parts.append(g[:, :GATE_GROUP])
        i_parts.append(g[:, GATE_GROUP:])
    r = _sigmoid(jnp.concatenate(r_parts, axis=1) + bg_ref[0:1, :])
    i = _sigmoid(jnp.concatenate(i_parts, axis=1) + bg_ref[1:2, :])
    neg_log_a = r * decay_ref[...]
    a = jnp.exp(-neg_log_a)
    one_minus_a2 = jnp.tanh(neg_log_a) * (a * a + 1.0)
    return a, jnp.sqrt(one_minus_a2) * (i * u)


def _scan_block(a, bx, h, reverse):
    steps = a.shape[0] // SUBLANES
    outs = [None] * steps
    order = range(steps - 1, -1, -1) if reverse else range(steps)
    for s in order:
        sl = slice(s * SUBLANES, (s + 1) * SUBLANES)
        h = a[sl, :] * h + bx[sl, :]
        outs[s] = h
    return jnp.concatenate(outs, axis=0), h


def _rnn_bwd_body(n_tiles, xp_ref, xm_ref, xn_ref, lng_ref, lnb_ref, wr_ref, br_ref, w4_ref, b4_ref,
                  wg_ref, bg_ref, decay_ref, u_ref, hb_ref, xr_scr, carry_scr):
    i = pl.program_id(0)
    tile = n_tiles - 1 - i
    lead = RNN_CONV_LEFT * SUBLANES

    @pl.when(i == 0)
    def _():
        carry_scr[...] = jnp.zeros_like(carry_scr)

    def xr_of(x_ref):
        return _dot(_ln(x_ref[...], lng_ref[...], lnb_ref[...]), wr_ref[...]) + br_ref[...]

    xr_scr[0:lead, :] = jnp.where(tile > 0, xr_of(xp_ref), 0.0)
    xr_scr[lead:lead + ROWS, :] = xr_of(xm_ref)
    xr_scr[lead + ROWS:lead + ROWS + SUBLANES, :] = jnp.where(tile < n_tiles - 1, xr_of(xn_ref), 0.0)

    h = carry_scr[...]
    for sb in range(ROWS // SUB_ROWS - 1, -1, -1):
        base = sb * SUB_ROWS
        u = jnp.broadcast_to(b4_ref[...], (SUB_ROWS, D_RNN))
        for k in range(RNN_CONV_W):
            u = u + w4_ref[k:k + 1, :] * xr_scr[base + k * SUBLANES:base + k * SUBLANES + SUB_ROWS, :]
        u_ref[base:base + SUB_ROWS, :] = u
        a, bx = _gate_terms(u, wg_ref, bg_ref, decay_ref)
        hs, h = _scan_block(a, bx, h, reverse=True)
        hb_ref[base:base + SUB_ROWS, :] = hs
    carry_scr[...] = h


def _rnn_fwd_body(xm_ref, u_ref, hb_ref, lng_ref, lnb_ref, wyg_ref, byg_ref, wg_ref, bg_ref, decay_ref,
                  wa_ref, ma_ref, z_scr, carry_scr):
    @pl.when(pl.program_id(0) == 0)
    def _():
        carry_scr[...] = jnp.zeros_like(carry_scr)

    hn = _ln(xm_ref[...], lng_ref[...], lnb_ref[...])
    p = _dot(hn, wyg_ref[...]) + byg_ref[...]
    h = carry_scr[...]
    for sb in range(ROWS // SUB_ROWS):
        rows = slice(sb * SUB_ROWS, (sb + 1) * SUB_ROWS)
        a, bx = _gate_terms(u_ref[rows, :], wg_ref, bg_ref, decay_ref)
        hs, h = _scan_block(a, bx, h, reverse=False)
        z_scr[rows, :] = (_gelu_tanh(p[rows, :D_RNN]) * (hs + hb_ref[rows, :])).astype(_BF16)
    carry_scr[...] = h
    y_a = jnp.dot(z_scr[...], wa_ref[...], preferred_element_type=_F32)
    ma_ref[...] = _sigmoid(p[:, D_RNN:]) * y_a


def _conv_mix_body(n_tiles, xp_ref, xm_ref, xn_ref, ma_ref, lng_ref, lnb_ref, wxc_ref, bxc_ref, wg1_ref, bg1_ref,
                   cw_ref, cb_ref, clg_ref, clb_ref, wb_ref, wout_ref, bout_ref, l1g_ref, l1b_ref,
                   h1_ref, c_scr, conv_scr):
    tile = pl.program_id(0)

    def glu_of(hn):
        xc = _dot(hn, wxc_ref[...]) + bxc_ref[...]
        return xc[:, :D_MODEL] * _sigmoid(xc[:, D_MODEL:])

    def ln_in(x_ref):
        return _ln(x_ref[...], lng_ref[...], lnb_ref[...])

    c_scr[0:HALO_ROWS, :] = jnp.where(tile > 0, glu_of(ln_in(xp_ref)), 0.0)
    hn = ln_in(xm_ref)
    c_scr[HALO_ROWS:HALO_ROWS + ROWS, :] = glu_of(hn)
    c_scr[HALO_ROWS + ROWS:, :] = jnp.where(tile < n_tiles - 1, glu_of(ln_in(xn_ref)), 0.0)

    first = HALO_ROWS - CONV_HALF * SUBLANES

    def conv_chunk(c, carry):
        base = pl.multiple_of(c * CONV_CHUNK, CONV_CHUNK)
        acc = jnp.broadcast_to(cb_ref[...], (CONV_CHUNK, D_MODEL))
        for k in range(CONV_W):
            acc = acc + cw_ref[k:k + 1, :] * c_scr[pl.ds(base + first + k * SUBLANES, CONV_CHUNK), :]
        conv_scr[pl.ds(base, CONV_CHUNK), :] = acc
        return carry

    jax.lax.fori_loop(0, ROWS // CONV_CHUNK, conv_chunk, 0)

    cn = _ln(conv_scr[...], clg_ref[...], clb_ref[...])
    y_b = _dot(cn * _sigmoid(cn), wb_ref[...])
    g1 = _sigmoid(_dot(hn, wg1_ref[...]) + bg1_ref[...])
    mixed = _dot(ma_ref[...] + g1 * y_b, wout_ref[...]) + bout_ref[...]
    h1_ref[...] = _ln(DEEPNORM_ALPHA * hn + mixed, l1g_ref[...], l1b_ref[...])


def _mlp_body(h1_ref, wup_ref, bup_ref, wdn_ref, bdn_ref, l2g_ref, l2b_ref, o_ref):
    h1 = h1_ref[...]
    m = jnp.maximum(_dot(h1, wup_ref[...]) + bup_ref[...], 0.0)
    y = _dot(m * m, wdn_ref[...]) + bdn_ref[...]
    o_ref[...] = _ln(DEEPNORM_ALPHA * h1 + y, l2g_ref[...], l2b_ref[...])


def _full(shape):
    return pl.BlockSpec(shape, lambda i: (0,) * len(shape))


def _row2(v):
    return v.reshape(1, -1).astype(_F32)


def _gate_weights(gate_w, gate_b):
    per_group = GATE_GROUP // RNN_BLOCK_W
    eye = jnp.eye(per_group, dtype=gate_w.dtype)
    groups = []
    for q in range(D_RNN // GATE_GROUP):
        blk = gate_w[:, q * per_group:(q + 1) * per_group]
        dense = jnp.einsum("gnio,nm->gnimo", blk, eye).reshape(2, GATE_GROUP, GATE_GROUP)
        groups.append(jnp.concatenate([dense[0], dense[1]], axis=1))
    return jnp.stack(groups).astype(_BF16), gate_b.reshape(2, D_RNN).astype(_F32)


def kernel(x, emb_ln_g, emb_ln_b, w_in, b_in, rnn_conv_w, rnn_conv_b, rg_gate_w, rg_gate_b, rg_a_param, w_branch_a, conv_w, conv_b, conv_ln_g, conv_ln_b, w_branch_b, w_out, b_out, ln1_g, ln1_b, w_up, b_up, w_down, b_down, ln2_g, ln2_b):
    batch, seq, _ = x.shape
    assert batch == SUBLANES and x.shape[2] == D_MODEL and w_in.shape[0] == 1
    n_rows = batch * seq
    assert n_rows % ROWS == 0 and ROWS % SUB_ROWS == 0 and ROWS % HALO_ROWS == 0
    n_tiles = n_rows // ROWS
    lead = RNN_CONV_LEFT * SUBLANES

    xt = jnp.transpose(x, (1, 0, 2)).reshape(n_rows, D_MODEL)

    lng, lnb = _row2(emb_ln_g), _row2(emb_ln_b)
    w_in0, b_in0 = w_in[0], b_in[0]
    o_yr, o_xc, o_gl = D_RNN, 2 * D_RNN, 2 * D_RNN + 2 * D_MODEL
    w_r = w_in0[:, :o_yr].astype(_BF16)
    b_r = _row2(b_in0[:o_yr])
    w_yg = jnp.concatenate([w_in0[:, o_yr:o_xc], w_in0[:, o_gl:o_gl + D_MODEL]], axis=1).astype(_BF16)
    b_yg = _row2(jnp.concatenate([b_in0[o_yr:o_xc], b_in0[o_gl:o_gl + D_MODEL]]))
    w_xc = w_in0[:, o_xc:o_gl].astype(_BF16)
    b_xc = _row2(b_in0[o_xc:o_gl])
    w_g1 = w_in0[:, o_gl + D_MODEL:].astype(_BF16)
    b_g1 = _row2(b_in0[o_gl + D_MODEL:])
    wg_f, bg_f = _gate_weights(rg_gate_w[0, 0], rg_gate_b[0, 0])
    wg_b, bg_b = _gate_weights(rg_gate_w[0, 1], rg_gate_b[0, 1])
    decay = RG_C * jax.nn.softplus(-rg_a_param[0].astype(_F32))
    params = pltpu.CompilerParams(dimension_semantics=("arbitrary",), vmem_limit_bytes=VMEM_LIMIT)

    rev = lambda i: n_tiles - 1 - i
    u, h_bwd = pl.pallas_call(
        functools.partial(_rnn_bwd_body, n_tiles),
        grid=(n_tiles,),
        in_specs=[
            pl.BlockSpec((lead, D_MODEL), lambda i: (jnp.maximum(rev(i) * (ROWS // lead) - 1, 0), 0)),
            pl.BlockSpec((ROWS, D_MODEL), lambda i: (rev(i), 0)),
            pl.BlockSpec((SUBLANES, D_MODEL),
                         lambda i: (jnp.minimum((rev(i) + 1) * (ROWS // SUBLANES), n_rows // SUBLANES - 1), 0)),
            _full((1, D_MODEL)), _full((1, D_MODEL)),
            _full((D_MODEL, D_RNN)), _full((1, D_RNN)),
            _full((RNN_CONV_W, D_RNN)), _full((1, D_RNN)),
            _full(wg_b.shape), _full((2, D_RNN)), _full((1, D_RNN)),
        ],
        out_specs=[pl.BlockSpec((ROWS, D_RNN), lambda i: (rev(i), 0)),
                   pl.BlockSpec((ROWS, D_RNN), lambda i: (rev(i), 0))],
        out_shape=[jax.ShapeDtypeStruct((n_rows, D_RNN), _F32)] * 2,
        scratch_shapes=[pltpu.VMEM((lead + ROWS + SUBLANES, D_RNN), _F32), pltpu.VMEM((SUBLANES, D_RNN), _F32)],
        compiler_params=params,
        name="rnn_bwd",
    )(xt, xt, xt, lng, lnb, w_r, b_r, rnn_conv_w[0].astype(_F32), _row2(rnn_conv_b[0]), wg_b, bg_b, decay[1:2])

    tile_spec = lambda d: pl.BlockSpec((ROWS, d), lambda i: (i, 0))
    merged_a = pl.pallas_call(
        _rnn_fwd_body,
        grid=(n_tiles,),
        in_specs=[
            tile_spec(D_MODEL), tile_spec(D_RNN), tile_spec(D_RNN),
            _full((1, D_MODEL)), _full((1, D_MODEL)),
            _full((D_MODEL, D_RNN + D_MODEL)), _full((1, D_RNN + D_MODEL)),
            _full(wg_f.shape), _full((2, D_RNN)), _full((1, D_RNN)),
            _full((D_RNN, D_MODEL)),
        ],
        out_specs=tile_spec(D_MODEL),
        out_shape=jax.ShapeDtypeStruct((n_rows, D_MODEL), _F32),
        scratch_shapes=[pltpu.VMEM((ROWS, D_RNN), _BF16), pltpu.VMEM((SUBLANES, D_RNN), _F32)],
        compiler_params=params,
        name="rnn_fwd",
    )(xt, u, h_bwd, lng, lnb, w_yg, b_yg, wg_f, bg_f, decay[0:1], w_branch_a[0].astype(_BF16))

    halo_per_tile = ROWS // HALO_ROWS
    h1 = pl.pallas_call(
        functools.partial(_conv_mix_body, n_tiles),
        grid=(n_tiles,),
        in_specs=[
            pl.BlockSpec((HALO_ROWS, D_MODEL), lambda i: (jnp.maximum(i * halo_per_tile - 1, 0), 0)),
            tile_spec(D_MODEL),
            pl.BlockSpec((HALO_ROWS, D_MODEL),
                         lambda i: (jnp.minimum((i + 1) * halo_per_tile, n_rows // HALO_ROWS - 1), 0)),
            tile_spec(D_MODEL),
            _full((1, D_MODEL)), _full((1, D_MODEL)),
            _full((D_MODEL, 2 * D_MODEL)), _full((1, 2 * D_MODEL)),
            _full((D_MODEL, D_MODEL)), _full((1, D_MODEL)),
            _full((CONV_W, D_MODEL)), _full((1, D_MODEL)),
            _full((1, D_MODEL)), _full((1, D_MODEL)),
            _full((D_MODEL, D_MODEL)),
            _full((D_MODEL, D_MODEL)), _full((1, D_MODEL)),
            _full((1, D_MODEL)), _full((1, D_MODEL)),
        ],
        out_specs=tile_spec(D_MODEL),
        out_shape=jax.ShapeDtypeStruct((n_rows, D_MODEL), _F32),
        scratch_shapes=[pltpu.VMEM((ROWS + 2 * HALO_ROWS, D_MODEL), _F32), pltpu.VMEM((ROWS, D_MODEL), _F32)],
        compiler_params=params,
        name="conv_mix",
    )(xt, xt, xt, merged_a, lng, lnb, w_xc, b_xc, w_g1, b_g1, conv_w[0].astype(_F32), _row2(conv_b[0]),
      _row2(conv_ln_g[0]), _row2(conv_ln_b[0]), w_branch_b[0].astype(_BF16),
      w_out[0].astype(_BF16), _row2(b_out[0]), _row2(ln1_g[0]), _row2(ln1_b[0]))

    out_t = pl.pallas_call(
        _mlp_body,
        grid=(n_tiles,),
        in_specs=[
            tile_spec(D_MODEL),
            _full((D_MODEL, D_FF)), _full((1, D_FF)),
            _full((D_FF, D_MODEL)), _full((1, D_MODEL)),
            _full((1, D_MODEL)), _full((1, D_MODEL)),
        ],
        out_specs=tile_spec(D_MODEL),
        out_shape=jax.ShapeDtypeStruct((n_rows, D_MODEL), _F32),
        compiler_params=params,
        name="mlp",
    )(h1, w_up[0].astype(_BF16), _row2(b_up[0]), w_down[0].astype(_BF16), _row2(b_down[0]),
      _row2(ln2_g[0]), _row2(ln2_b[0]))

    return jnp.transpose(out_t.reshape(seq, batch, D_MODEL), (1, 0, 2))
```

```python
import functools

import jax
import jax.numpy as jnp
from jax.experimental import pallas as pl
from jax.experimental.pallas import tpu as pltpu

D_MODEL = 1024
D_RNN = 1536
RNN_BLOCKS = 16
RNN_BLOCK_W = D_RNN // RNN_BLOCKS
GATE_GROUP = 768
RNN_CONV_W = 4
RNN_CONV_LEFT = 2
RG_C = 8.0
CONV_W = 31
CONV_HALF = (CONV_W - 1) // 2
D_FF = 4 * D_MODEL
LN_EPS = 1e-5
DEEPNORM_ALPHA = 2.0 ** 0.25

SUBLANES = 8
ROWS = 512
SUB_ROWS = 256
HALF_ROWS = ROWS // 2
LANES = 128
CONV_ROWS = 32
XC_COL = 256
XC_ROW_SPLIT = 2
HALO_ROWS = 128
VMEM_LIMIT = 56 * 1024 * 1024

_BF16 = jnp.bfloat16
_F32 = jnp.float32
_GELU_C = 0.7978845608028654
_F32_TINY = 1.1754943508222875e-38


def _ln(x, g, b):
    mu = jnp.mean(x, axis=-1, keepdims=True)
    xc = x - mu
    var = jnp.mean(xc * xc, axis=-1, keepdims=True)
    return xc * jax.lax.rsqrt(var + LN_EPS) * g + b


def _sigmoid(x):
    return 0.5 * jnp.tanh(0.5 * x) + 0.5


def _gelu_tanh(x):
    hx = 0.5 * x
    t = jnp.tanh(x * ((x * x) * (_GELU_C * 0.044715) + _GELU_C))
    return hx * t + hx


def _dot(a, w):
    return jnp.dot(a.astype(_BF16), w, preferred_element_type=_F32)


def _gate_terms(u, wg_ref, bg_ref, nhd_ref):
    ub = u.astype(_BF16)
    r_parts, i_parts = [], []
    for q in range(D_RNN // GATE_GROUP):
        g = jnp.dot(ub[:, q * GATE_GROUP:(q + 1) * GATE_GROUP], wg_ref[q],
                    preferred_element_type=_F32)
        r_parts.append(g[:, :GATE_GROUP])
        i_parts.append(g[:, GATE_GROUP:])
    t_r = jnp.tanh(jnp.concatenate(r_parts, axis=1) + bg_ref[0:1, :])
    t_i = jnp.tanh(jnp.concatenate(i_parts, axis=1) + bg_ref[1:2, :])
    nhd = nhd_ref[...]
    log_a = t_r * nhd + nhd
    a = jnp.exp(log_a)
    q2 = jnp.tanh(log_a) * (-1.0 - a * a)
    root = q2 * jax.lax.rsqrt(jnp.maximum(q2, _F32_TINY))
    return a, root * ((0.5 * t_i + 0.5) * u)


def _scan_block(a, bx, h, reverse):
    steps = a.shape[0] // SUBLANES
    outs = [None] * steps
    order = range(steps - 1, -1, -1) if reverse else range(steps)
    for s in order:
        sl = slice(s * SUBLANES, (s + 1) * SUBLANES)
        h = a[sl, :] * h + bx[sl, :]
        outs[s] = h
    return jnp.concatenate(outs, axis=0), h


def _rnn_bwd_body(n_tiles, xp_ref, xm_ref, xn_ref, lng_ref, lnb_ref, wr_ref, br_ref, w4_ref, b4_ref,
                  wg_ref, bg_ref, decay_ref, u_ref, hb_ref, xr_scr, carry_scr):
    i = pl.program_id(0)
    tile = n_tiles - 1 - i
    lead = RNN_CONV_LEFT * SUBLANES

    @pl.when(i == 0)
    def _():
        carry_scr[...] = jnp.zeros_like(carry_scr)

    def xr_of(x_ref):
        return _dot(_ln(x_ref[...], lng_ref[...], lnb_ref[...]), wr_ref[...]) + br_ref[...]

    xr_scr[0:lead, :] = jnp.where(tile > 0, xr_of(xp_ref), 0.0)
    xr_scr[lead:lead + ROWS, :] = xr_of(xm_ref)
    xr_scr[lead + ROWS:lead + ROWS + SUBLANES, :] = jnp.where(tile < n_tiles - 1, xr_of(xn_ref), 0.0)

    h = carry_scr[...]
    for sb in range(ROWS // SUB_ROWS - 1, -1, -1):
        base = sb * SUB_ROWS
        u = jnp.broadcast_to(b4_ref[...], (SUB_ROWS, D_RNN))
        for k in range(RNN_CONV_W):
            u = u + w4_ref[k:k + 1, :] * xr_scr[base + k * SUBLANES:base + k * SUBLANES + SUB_ROWS, :]
        u_ref[base:base + SUB_ROWS, :] = u
        a, bx = _gate_terms(u, wg_ref, bg_ref, decay_ref)
        hs, h = _scan_block(a, bx, h, reverse=True)
        hb_ref[base:base + SUB_ROWS, :] = hs
    carry_scr[...] = h


def _rnn_fwd_body(xm_ref, u_ref, hb_ref, lng_ref, lnb_ref, wyg_ref, byg_ref, wg_ref, bg_ref, decay_ref,
                  wa_ref, ma_ref, z_scr, carry_scr):
    @pl.when(pl.program_id(0) == 0)
    def _():
        carry_scr[...] = jnp.zeros_like(carry_scr)

    hn = _ln(xm_ref[...], lng_ref[...], lnb_ref[...])
    p = _dot(hn, wyg_ref[...]) + byg_ref[...]
    h = carry_scr[...]
    for sb in range(ROWS // SUB_ROWS):
        rows = slice(sb * SUB_ROWS, (sb + 1) * SUB_ROWS)
        a, bx = _gate_terms(u_ref[rows, :], wg_ref, bg_ref, decay_ref)
        hs, h = _scan_block(a, bx, h, reverse=False)
        z_scr[rows, :] = (_gelu_tanh(p[rows, :D_RNN]) * (hs + hb_ref[rows, :])).astype(_BF16)
    carry_scr[...] = h
    y_a = jnp.dot(z_scr[...], wa_ref[...], preferred_element_type=_F32)
    ma_ref[...] = _sigmoid(p[:, D_RNN:]) * y_a


def _conv_mix_body(n_tiles, xm_ref, xn_ref, ma_ref, lng_ref, lnb_ref, wxc_ref, bxc_ref, wg1_ref, bg1_ref,
                   cw_ref, cb_ref, clg_ref, clb_ref, wb_ref, wout_ref, bout_ref, l1g_ref, l1b_ref,
                   h1_ref, hnb_scr, xc_scr, c_new, c_old, hn_new, hn_old, conv_scr):
    s = pl.program_id(0)

    @pl.when(s == 0)
    def _():
        c_old[...] = jnp.zeros_like(c_old)
        hn_old[...] = jnp.zeros_like(hn_old)

    hn = _ln(xm_ref[...], lng_ref[...], lnb_ref[...])
    hn_new[...] = hn
    hnb_scr[0:ROWS, :] = hn.astype(_BF16)
    hnb_scr[ROWS:, :] = _ln(xn_ref[...], lng_ref[...], lnb_ref[...]).astype(_BF16)

    first = HALO_ROWS - CONV_HALF * SUBLANES
    groups = CONV_ROWS // SUBLANES
    n_col = wxc_ref.shape[0]
    m_rows = (ROWS + HALO_ROWS) // XC_ROW_SPLIT
    never = s < 0

    def piece(i, carry):
        col = i % n_col
        r0 = pl.multiple_of((i // n_col) * m_rows, m_rows)
        xc_scr[col, pl.ds(r0, m_rows), :] = jnp.dot(hnb_scr[pl.ds(r0, m_rows), :], wxc_ref[col],
                                                    preferred_element_type=_F32)
        base = pl.multiple_of(i * CONV_ROWS, CONV_ROWS)
        prev = None
        for l0 in range(0, D_MODEL, LANES):
            lanes = slice(l0, l0 + LANES)
            bias = jnp.broadcast_to(cb_ref[:, lanes], (SUBLANES, LANES))
            acc = [bias if prev is None else jnp.where(never, prev[g], bias) for g in range(groups)]
            for k in range(CONV_W):
                w = cw_ref[k * SUBLANES:(k + 1) * SUBLANES, lanes]
                for g in range(groups):
                    acc[g] = acc[g] + w * c_old[pl.ds(base + first + (k + g) * SUBLANES, SUBLANES), lanes]
            for g in range(groups):
                conv_scr[pl.ds(base + g * SUBLANES, SUBLANES), lanes] = acc[g]
            prev = acc
        return carry

    jax.lax.fori_loop(0, ROWS // CONV_ROWS, piece, 0)

    half_cols = n_col // 2
    for t in range(half_cols):
        lanes = slice(t * XC_COL, (t + 1) * XC_COL)
        val = xc_scr[t] + bxc_ref[:, lanes]
        gate = xc_scr[t + half_cols] + bxc_ref[:, D_MODEL + t * XC_COL:D_MODEL + (t + 1) * XC_COL]
        c = val * _sigmoid(gate)
        c_new[HALO_ROWS:HALO_ROWS + ROWS, lanes] = c[:ROWS]
        c_new[HALO_ROWS + ROWS:, lanes] = jnp.where(s < n_tiles - 1, c[ROWS:], 0.0)
    c_new[0:HALO_ROWS, :] = jnp.where(s > 0, c_old[ROWS:ROWS + HALO_ROWS, :], 0.0)

    for r0 in range(0, ROWS, HALF_ROWS):
        rows = slice(r0, r0 + HALF_ROWS)
        hp = hn_old[rows, :]
        cn = _ln(conv_scr[rows, :], clg_ref[...], clb_ref[...])
        y_b = _dot(cn * _sigmoid(cn), wb_ref[...])
        g1 = _sigmoid(_dot(hp, wg1_ref[...]) + bg1_ref[...])
        mixed = _dot(ma_ref[rows, :] + g1 * y_b, wout_ref[...]) + bout_ref[...]
        h1_ref[rows, :] = _ln(DEEPNORM_ALPHA * hp + mixed, l1g_ref[...], l1b_ref[...])

    c_old[...] = c_new[...]
    hn_old[...] = hn_new[...]


def _mlp_body(h1_ref, wup_ref, bup_ref, wdn_ref, bdn_ref, l2g_ref, l2b_ref, o_ref):
    h1 = h1_ref[...]
    m = jnp.maximum(_dot(h1, wup_ref[...]) + bup_ref[...], 0.0)
    y = _dot(m * m, wdn_ref[...]) + bdn_ref[...]
    o_ref[...] = _ln(DEEPNORM_ALPHA * h1 + y, l2g_ref[...], l2b_ref[...])


def _full(shape):
    return pl.BlockSpec(shape, lambda i: (0,) * len(shape), pipeline_mode=pl.Buffered(1))


def _row2(v):
    return v.reshape(1, -1).astype(_F32)


def _gate_weights(gate_w, gate_b):
    per_group = GATE_GROUP // RNN_BLOCK_W
    eye = jnp.eye(per_group, dtype=gate_w.dtype)
    groups = []
    for q in range(D_RNN // GATE_GROUP):
        blk = gate_w[:, q * per_group:(q + 1) * per_group]
        dense = jnp.einsum("gnio,nm->gnimo", blk, eye).reshape(2, GATE_GROUP, GATE_GROUP)
        groups.append(jnp.concatenate([dense[0], dense[1]], axis=1))
    return (0.5 * jnp.stack(groups)).astype(_BF16), 0.5 * gate_b.reshape(2, D_RNN).astype(_F32)


def kernel(x, emb_ln_g, emb_ln_b, w_in, b_in, rnn_conv_w, rnn_conv_b, rg_gate_w, rg_gate_b, rg_a_param, w_branch_a, conv_w, conv_b, conv_ln_g, conv_ln_b, w_branch_b, w_out, b_out, ln1_g, ln1_b, w_up, b_up, w_down, b_down, ln2_g, ln2_b):
    batch, seq, _ = x.shape
    assert batch == SUBLANES and x.shape[2] == D_MODEL and w_in.shape[0] == 1
    n_rows = batch * seq
    assert n_rows % ROWS == 0 and ROWS % SUB_ROWS == 0 and ROWS % HALO_ROWS == 0
    n_tiles = n_rows // ROWS
    lead = RNN_CONV_LEFT * SUBLANES

    xt = jnp.transpose(x, (1, 0, 2)).reshape(n_rows, D_MODEL)

    lng, lnb = _row2(emb_ln_g), _row2(emb_ln_b)
    w_in0, b_in0 = w_in[0], b_in[0]
    o_yr, o_xc, o_gl = D_RNN, 2 * D_RNN, 2 * D_RNN + 2 * D_MODEL
    w_r = w_in0[:, :o_yr].astype(_BF16)
    b_r = _row2(b_in0[:o_yr])
    w_yg = jnp.concatenate([w_in0[:, o_yr:o_xc], w_in0[:, o_gl:o_gl + D_MODEL]], axis=1).astype(_BF16)
    b_yg = _row2(jnp.concatenate([b_in0[o_yr:o_xc], b_in0[o_gl:o_gl + D_MODEL]]))
    w_xc = w_in0[:, o_xc:o_gl].astype(_BF16)
    b_xc = _row2(b_in0[o_xc:o_gl])
    w_g1 = w_in0[:, o_gl + D_MODEL:].astype(_BF16)
    b_g1 = _row2(b_in0[o_gl + D_MODEL:])
    wg_f, bg_f = _gate_weights(rg_gate_w[0, 0], rg_gate_b[0, 0])
    wg_b, bg_b = _gate_weights(rg_gate_w[0, 1], rg_gate_b[0, 1])
    decay = -0.5 * RG_C * jax.nn.softplus(-rg_a_param[0].astype(_F32))
    params = pltpu.CompilerParams(dimension_semantics=("arbitrary",), vmem_limit_bytes=VMEM_LIMIT)

    rev = lambda i: n_tiles - 1 - i
    u, h_bwd = pl.pallas_call(
        functools.partial(_rnn_bwd_body, n_tiles),
        grid=(n_tiles,),
        in_specs=[
            pl.BlockSpec((lead, D_MODEL), lambda i: (jnp.maximum(rev(i) * (ROWS // lead) - 1, 0), 0)),
            pl.BlockSpec((ROWS, D_MODEL), lambda i: (rev(i), 0)),
            pl.BlockSpec((SUBLANES, D_MODEL),
                         lambda i: (jnp.minimum((rev(i) + 1) * (ROWS // SUBLANES), n_rows // SUBLANES - 1), 0)),
            _full((1, D_MODEL)), _full((1, D_MODEL)),
            _full((D_MODEL, D_RNN)), _full((1, D_RNN)),
            _full((RNN_CONV_W, D_RNN)), _full((1, D_RNN)),
            _full(wg_b.shape), _full((2, D_RNN)), _full((1, D_RNN)),
        ],
        out_specs=[pl.BlockSpec((ROWS, D_RNN), lambda i: (rev(i), 0)),
                   pl.BlockSpec((ROWS, D_RNN), lambda i: (rev(i), 0))],
        out_shape=[jax.ShapeDtypeStruct((n_rows, D_RNN), _F32)] * 2,
        scratch_shapes=[pltpu.VMEM((lead + ROWS + SUBLANES, D_RNN), _F32), pltpu.VMEM((SUBLANES, D_RNN), _F32)],
        compiler_params=params,
        name="rnn_bwd",
    )(xt, xt, xt, lng, lnb, w_r, b_r, rnn_conv_w[0].astype(_F32), _row2(rnn_conv_b[0]), wg_b, bg_b, decay[1:2])

    tile_spec = lambda d: pl.BlockSpec((ROWS, d), lambda i: (i, 0))
    merged_a = pl.pallas_call(
        _rnn_fwd_body,
        grid=(n_tiles,),
        in_specs=[
            tile_spec(D_MODEL), tile_spec(D_RNN), tile_spec(D_RNN),
            _full((1, D_MODEL)), _full((1, D_MODEL)),
            _full((D_MODEL, D_RNN + D_MODEL)), _full((1, D_RNN + D_MODEL)),
            _full(wg_f.shape), _full((2, D_RNN)), _full((1, D_RNN)),
            _full((D_RNN, D_MODEL)),
        ],
        out_specs=tile_spec(D_MODEL),
        out_shape=jax.ShapeDtypeStruct((n_rows, D_MODEL), _F32),
        scratch_shapes=[pltpu.VMEM((ROWS, D_RNN), _BF16), pltpu.VMEM((SUBLANES, D_RNN), _F32)],
        compiler_params=params,
        name="rnn_fwd",
    )(xt, u, h_bwd, lng, lnb, w_yg, b_yg, wg_f, bg_f, decay[0:1], w_branch_a[0].astype(_BF16))

    halo_per_tile = ROWS // HALO_ROWS
    cur_tile = lambda s: jnp.minimum(s, n_tiles - 1)
    prev_tile = lambda s: jnp.maximum(s - 1, 0)
    conv_taps = jnp.repeat(conv_w[0].astype(_F32), SUBLANES, axis=0)
    c_rows = ROWS + 2 * HALO_ROWS
    n_col = 2 * D_MODEL // XC_COL
    assert ROWS // CONV_ROWS == n_col * XC_ROW_SPLIT and (ROWS + HALO_ROWS) % (16 * XC_ROW_SPLIT) == 0
    w_xc_tiles = jnp.transpose(w_xc.reshape(D_MODEL, n_col, XC_COL), (1, 0, 2))
    h1 = pl.pallas_call(
        functools.partial(_conv_mix_body, n_tiles),
        grid=(n_tiles + 1,),
        in_specs=[
            pl.BlockSpec((ROWS, D_MODEL), lambda s: (cur_tile(s), 0)),
            pl.BlockSpec((HALO_ROWS, D_MODEL),
                         lambda s: (jnp.minimum((cur_tile(s) + 1) * halo_per_tile, n_rows // HALO_ROWS - 1), 0)),
            pl.BlockSpec((ROWS, D_MODEL), lambda s: (prev_tile(s), 0)),
            _full((1, D_MODEL)), _full((1, D_MODEL)),
            _full((n_col, D_MODEL, XC_COL)), _full((1, 2 * D_MODEL)),
            _full((D_MODEL, D_MODEL)), _full((1, D_MODEL)),
            _full((CONV_W * SUBLANES, D_MODEL)), _full((1, D_MODEL)),
            _full((1, D_MODEL)), _full((1, D_MODEL)),
            _full((D_MODEL, D_MODEL)),
            _full((D_MODEL, D_MODEL)), _full((1, D_MODEL)),
            _full((1, D_MODEL)), _full((1, D_MODEL)),
        ],
        out_specs=pl.BlockSpec((ROWS, D_MODEL), lambda s: (prev_tile(s), 0)),
        out_shape=jax.ShapeDtypeStruct((n_rows, D_MODEL), _F32),
        scratch_shapes=[pltpu.VMEM((ROWS + HALO_ROWS, D_MODEL), _BF16),
                        pltpu.VMEM((n_col, ROWS + HALO_ROWS, XC_COL), _F32),
                        pltpu.VMEM((c_rows, D_MODEL), _F32), pltpu.VMEM((c_rows, D_MODEL), _F32),
                        pltpu.VMEM((ROWS, D_MODEL), _F32), pltpu.VMEM((ROWS, D_MODEL), _F32),
                        pltpu.VMEM((ROWS, D_MODEL), _F32)],
        compiler_params=params,
        name="conv_mix",
    )(xt, xt, merged_a, lng, lnb, w_xc_tiles, b_xc, w_g1, b_g1, conv_taps, _row2(conv_b[0]),
      _row2(conv_ln_g[0]), _row2(conv_ln_b[0]), w_branch_b[0].astype(_BF16),
      w_out[0].astype(_BF16), _row2(b_out[0]), _row2(ln1_g[0]), _row2(ln1_b[0]))

    out_t = pl.pallas_call(
        _mlp_body,
        grid=(n_tiles,),
        in_specs=[
            tile_spec(D_MODEL),
            _full((D_MODEL, D_FF)), _full((1, D_FF)),
            _full((D_FF, D_MODEL)), _full((1, D_MODEL)),
            _full((1, D_MODEL)), _full((1, D_MODEL)),
        ],
        out_specs=tile_spec(D_MODEL),
        out_shape=jax.ShapeDtypeStruct((n_rows, D_MODEL), _F32),
        compiler_params=params,
        name="mlp",
    )(h1, w_up[0].astype(_BF16), _row2(b_up[0]), w_down[0].astype(_BF16), _row2(b_down[0]),
      _row2(ln2_g[0]), _row2(ln2_b[0]))

    return jnp.transpose(out_t.reshape(seq, batch, D_MODEL), (1, 0, 2))
```

```python
import functools

import jax
import jax.numpy as jnp
from jax.experimental import pallas as pl
from jax.experimental.pallas import tpu as pltpu

D_MODEL = 1024
D_RNN = 1536
RNN_BLOCKS = 16
RNN_BLOCK_W = D_RNN // RNN_BLOCKS
GATE_GROUP = 768
RNN_CONV_W = 4
RNN_CONV_LEFT = 2
RNN_CONV_RIGHT = RNN_CONV_W - 1 - RNN_CONV_LEFT
RG_C = 8.0
CONV_W = 31
CONV_HALF = (CONV_W - 1) // 2
D_FF = 4 * D_MODEL
LN_EPS = 1e-5
DEEPNORM_ALPHA = 2.0 ** 0.25

SUBLANES = 8
ROWS = 512
STEPS = ROWS // SUBLANES
SUB_ROWS = 256
HALF_ROWS = ROWS // 2
LANES = 128
MXU_COL = 256
CONV_ROWS = 64
CONV_CHUNKS = ROWS // CONV_ROWS
HALO_ROWS = 128
VMEM_LIMIT = 56 * 1024 * 1024

_BF16 = jnp.bfloat16
_F32 = jnp.float32
_GELU_C = 0.7978845608028654
_F32_TINY = 1.1754943508222875e-38


def _ln(x, g, b):
    mu = jnp.mean(x, axis=-1, keepdims=True)
    xc = x - mu
    var = jnp.mean(xc * xc, axis=-1, keepdims=True)
    return xc * jax.lax.rsqrt(var + LN_EPS) * g + b


def _sigmoid(x):
    return 0.5 * jnp.tanh(0.5 * x) + 0.5


def _gelu_tanh(x):
    hx = 0.5 * x
    t = jnp.tanh(x * ((x * x) * (_GELU_C * 0.044715) + _GELU_C))
    return hx * t + hx


def _dot(a, w):
    return jnp.dot(a.astype(_BF16), w, preferred_element_type=_F32)


def _gate_terms(u, wg_refs, bg_ref, nhd_ref):
    wlo_ref, wmid_ref, whi_ref = wg_refs
    ub = u.astype(_BF16)
    r_parts, i_parts = [], []
    for q in range(D_RNN // GATE_GROUP):
        k0 = q * GATE_GROUP
        lo = jnp.dot(ub[:, k0:k0 + 2 * MXU_COL], wlo_ref[q], preferred_element_type=_F32)
        mid = jnp.dot(ub[:, k0:k0 + 3 * MXU_COL], wmid_ref[q], preferred_element_type=_F32)
        hi = jnp.dot(ub[:, k0 + MXU_COL:k0 + 3 * MXU_COL], whi_ref[q], preferred_element_type=_F32)
        for part in (lo, mid, hi):
            r_parts.append(part[:, :MXU_COL])
            i_parts.append(part[:, MXU_COL:])
    t_r = jnp.tanh(jnp.concatenate(r_parts, axis=1) + bg_ref[0:1, :])
    t_i = jnp.tanh(jnp.concatenate(i_parts, axis=1) + bg_ref[1:2, :])
    nhd = nhd_ref[...]
    log_a = t_r * nhd + nhd
    a = jnp.exp(log_a)
    q2 = jnp.tanh(log_a) * (-1.0 - a * a)
    root = q2 * jax.lax.rsqrt(jnp.maximum(q2, _F32_TINY))
    return a, root * ((0.5 * t_i + 0.5) * u)


def _scan_block(a, bx, h, reverse):
    steps = a.shape[0] // SUBLANES
    outs = [None] * steps
    order = range(steps - 1, -1, -1) if reverse else range(steps)
    for s in order:
        sl = slice(s * SUBLANES, (s + 1) * SUBLANES)
        h = a[sl, :] * h + bx[sl, :]
        outs[s] = h
    return jnp.concatenate(outs, axis=0), h


def _x_tile_copies(x_hbm, xbuf, sems, n_tiles, tile, slot, start):
    t0 = tile * STEPS

    def run(src_start, n_steps, dst_start, sem):
        for b in range(SUBLANES):
            cp = pltpu.make_async_copy(x_hbm.at[b, pl.ds(src_start, n_steps), :],
                                       xbuf.at[slot, pl.ds(dst_start, n_steps), b, :], sem)
            if start:
                cp.start()
            else:
                cp.wait()

    run(t0, STEPS, RNN_CONV_LEFT, sems.at[slot, 0])

    @pl.when(tile > 0)
    def _():
        run(t0 - RNN_CONV_LEFT, RNN_CONV_LEFT, 0, sems.at[slot, 1])

    @pl.when(tile < n_tiles - 1)
    def _():
        run(t0 + STEPS, RNN_CONV_RIGHT, RNN_CONV_LEFT + STEPS, sems.at[slot, 2])


def _rnn_bwd_body(n_tiles, x_hbm, lng_ref, lnb_ref, wr_ref, br_ref, w4_ref, b4_ref,
                  wlo_ref, wmid_ref, whi_ref, bg_ref, decay_ref, u_ref, hb_ref, hn_ref,
                  xbuf, sems, xr_scr, carry_scr):
    wg_ref = (wlo_ref, wmid_ref, whi_ref)
    i = pl.program_id(0)
    tile = n_tiles - 1 - i
    slot = i % 2
    lead = RNN_CONV_LEFT * SUBLANES
    ext_rows = (RNN_CONV_LEFT + STEPS + RNN_CONV_RIGHT) * SUBLANES

    @pl.when(i == 0)
    def _():
        carry_scr[...] = jnp.zeros_like(carry_scr)
        xbuf[...] = jnp.zeros_like(xbuf)
        _x_tile_copies(x_hbm, xbuf, sems, n_tiles, tile, slot, start=True)

    @pl.when(i + 1 < n_tiles)
    def _():
        _x_tile_copies(x_hbm, xbuf, sems, n_tiles, tile - 1, 1 - slot, start=True)

    _x_tile_copies(x_hbm, xbuf, sems, n_tiles, tile, slot, start=False)

    hn = _ln(xbuf[slot].reshape(ext_rows, D_MODEL), lng_ref[...], lnb_ref[...])
    hn_ref[...] = hn[lead:lead + ROWS, :]
    xr = _dot(hn, wr_ref[...]) + br_ref[...]
    xr_scr[0:lead, :] = jnp.where(tile > 0, xr[0:lead, :], 0.0)
    xr_scr[lead:lead + ROWS, :] = xr[lead:lead + ROWS, :]
    xr_scr[lead + ROWS:, :] = jnp.where(tile < n_tiles - 1, xr[lead + ROWS:, :], 0.0)

    h = carry_scr[...]
    for sb in range(ROWS // SUB_ROWS - 1, -1, -1):
        base = sb * SUB_ROWS
        u = jnp.broadcast_to(b4_ref[...], (SUB_ROWS, D_RNN))
        for k in range(RNN_CONV_W):
            u = u + w4_ref[k:k + 1, :] * xr_scr[base + k * SUBLANES:base + k * SUBLANES + SUB_ROWS, :]
        u_ref[base:base + SUB_ROWS, :] = u
        a, bx = _gate_terms(u, wg_ref, bg_ref, decay_ref)
        hs, h = _scan_block(a, bx, h, reverse=True)
        hb_ref[base:base + SUB_ROWS, :] = hs
    carry_scr[...] = h


def _rnn_fwd_body(hn_ref, u_ref, hb_ref, wyg_ref, byg_ref, wlo_ref, wmid_ref, whi_ref, bg_ref, decay_ref,
                  wa_ref, ma_ref, z_scr, carry_scr):
    wg_ref = (wlo_ref, wmid_ref, whi_ref)

    @pl.when(pl.program_id(0) == 0)
    def _():
        carry_scr[...] = jnp.zeros_like(carry_scr)

    p = _dot(hn_ref[...], wyg_ref[...]) + byg_ref[...]
    h = carry_scr[...]
    for sb in range(ROWS // SUB_ROWS):
        rows = slice(sb * SUB_ROWS, (sb + 1) * SUB_ROWS)
        a, bx = _gate_terms(u_ref[rows, :], wg_ref, bg_ref, decay_ref)
        hs, h = _scan_block(a, bx, h, reverse=False)
        z_scr[rows, :] = (_gelu_tanh(p[rows, :D_RNN]) * (hs + hb_ref[rows, :])).astype(_BF16)
    carry_scr[...] = h
    y_a = jnp.dot(z_scr[...], wa_ref[...], preferred_element_type=_F32)
    ma_ref[...] = _sigmoid(p[:, D_RNN:]) * y_a


def _conv_mix_body(n_tiles, hc_ref, hh_ref, hp_ref, ma_ref, wxc_ref, bxc_ref, wg1_ref, bg1_ref,
                   cw_ref, cb_ref, clg_ref, clb_ref, wb_ref, wout_ref, bout_ref, l1g_ref, l1b_ref,
                   h1_ref, hcb_scr, hpb_scr, xc_scr, g1_scr, c_new, c_old, conv_scr):
    s = pl.program_id(0)
    n_lane_cols = D_MODEL // LANES

    @pl.when(s == 0)
    def _():
        c_old[...] = jnp.zeros_like(c_old)

    hcb_scr[0:ROWS, :] = hc_ref[...].astype(_BF16)
    hcb_scr[ROWS:, :] = hh_ref[...].astype(_BF16)
    hpb_scr[...] = hp_ref[...].astype(_BF16)

    first = HALO_ROWS - CONV_HALF * SUBLANES
    groups = CONV_ROWS // SUBLANES
    n_col = wxc_ref.shape[0]
    m_rows = (ROWS + HALO_ROWS) // (CONV_CHUNKS // n_col)
    g_col = wg1_ref.shape[0]
    g_rows = ROWS // (CONV_CHUNKS // g_col)
    never = s < 0

    def piece(i, carry):
        col = i % n_col
        r0 = pl.multiple_of((i // n_col) * m_rows, m_rows)
        xc_scr[col, pl.ds(r0, m_rows), :] = jnp.dot(hcb_scr[pl.ds(r0, m_rows), :], wxc_ref[col],
                                                    preferred_element_type=_F32)
        gc = i % g_col
        q0 = pl.multiple_of((i // g_col) * g_rows, g_rows)
        g1_scr[gc, pl.ds(q0, g_rows), :] = jnp.dot(hpb_scr[pl.ds(q0, g_rows), :], wg1_ref[gc],
                                                   preferred_element_type=_F32)
        base = pl.multiple_of(i * CONV_ROWS, CONV_ROWS)
        prev = None
        for l in range(n_lane_cols):
            bias = jnp.broadcast_to(cb_ref[:, l * LANES:(l + 1) * LANES], (SUBLANES, LANES))
            acc = [bias if prev is None else jnp.where(never, prev[g], bias) for g in range(groups)]
            for k in range(CONV_W):
                w = cw_ref[l, k * SUBLANES:(k + 1) * SUBLANES, :]
                for g in range(groups):
                    acc[g] = acc[g] + w * c_old[l, pl.ds(base + first + (k + g) * SUBLANES, SUBLANES), :]
            for g in range(groups):
                conv_scr[l, pl.ds(base + g * SUBLANES, SUBLANES), :] = acc[g]
            prev = acc
        return carry

    jax.lax.fori_loop(0, CONV_CHUNKS, piece, 0)

    half_cols = n_col // 2
    per_tile = MXU_COL // LANES
    for t in range(half_cols):
        val = xc_scr[t] + bxc_ref[:, t * MXU_COL:(t + 1) * MXU_COL]
        gate = xc_scr[t + half_cols] + bxc_ref[:, D_MODEL + t * MXU_COL:D_MODEL + (t + 1) * MXU_COL]
        c = val * _sigmoid(gate)
        for j in range(per_tile):
            cj = c[:, j * LANES:(j + 1) * LANES]
            c_new[t * per_tile + j, HALO_ROWS:HALO_ROWS + ROWS, :] = cj[:ROWS]
            c_new[t * per_tile + j, HALO_ROWS + ROWS:, :] = jnp.where(s < n_tiles - 1, cj[ROWS:], 0.0)
    c_new[:, 0:HALO_ROWS, :] = jnp.where(s > 0, c_old[:, ROWS:ROWS + HALO_ROWS, :], 0.0)

    for r0 in range(0, ROWS, HALF_ROWS):
        rows = slice(r0, r0 + HALF_ROWS)
        hp = hp_ref[rows, :]
        conv = jnp.concatenate([conv_scr[l, rows, :] for l in range(n_lane_cols)], axis=1)
        cn = _ln(conv, clg_ref[...], clb_ref[...])
        y_b = _dot(cn * _sigmoid(cn), wb_ref[...])
        g1 = _sigmoid(jnp.concatenate([g1_scr[t, rows, :] for t in range(g_col)], axis=1) + bg1_ref[...])
        mixed = _dot(ma_ref[rows, :] + g1 * y_b, wout_ref[...]) + bout_ref[...]
        h1_ref[rows, :] = _ln(DEEPNORM_ALPHA * hp + mixed, l1g_ref[...], l1b_ref[...])

    c_old[...] = c_new[...]


def _out_tile_copies(obuf, out_hbm, sems, tile, slot, start):
    for b in range(SUBLANES):
        cp = pltpu.make_async_copy(obuf.at[slot, :, b, :], out_hbm.at[b, pl.ds(tile * STEPS, STEPS), :],
                                   sems.at[slot])
        if start:
            cp.start()
        else:
            cp.wait()


def _mlp_body(n_tiles, h1_ref, wup_ref, bup_ref, wdn_ref, bdn_ref, l2g_ref, l2b_ref, out_hbm, obuf, sems):
    i = pl.program_id(0)
    slot = i % 2

    @pl.when(i >= 2)
    def _():
        _out_tile_copies(obuf, out_hbm, sems, i - 2, slot, start=False)

    h1 = h1_ref[...]
    m = jnp.maximum(_dot(h1, wup_ref[...]) + bup_ref[...], 0.0)
    y = _dot(m * m, wdn_ref[...]) + bdn_ref[...]
    res = _ln(DEEPNORM_ALPHA * h1 + y, l2g_ref[...], l2b_ref[...])
    obuf[slot] = res.reshape(STEPS, SUBLANES, D_MODEL)
    _out_tile_copies(obuf, out_hbm, sems, i, slot, start=True)

    @pl.when(i == n_tiles - 1)
    def _():
        if n_tiles >= 2:
            _out_tile_copies(obuf, out_hbm, sems, i - 1, 1 - slot, start=False)
        _out_tile_copies(obuf, out_hbm, sems, i, slot, start=False)


def _full(shape):
    return pl.BlockSpec(shape, lambda i: (0,) * len(shape), pipeline_mode=pl.Buffered(1))


def _row2(v):
    return v.reshape(1, -1).astype(_F32)


def _gate_weights(gate_w, gate_b):
    t = MXU_COL
    per_group = GATE_GROUP // RNN_BLOCK_W
    assert GATE_GROUP == 3 * t and per_group * RNN_BLOCK_W == GATE_GROUP
    assert (-(-t // RNN_BLOCK_W)) * RNN_BLOCK_W <= 2 * t
    eye = jnp.eye(per_group, dtype=gate_w.dtype)
    lo, mid, hi = [], [], []
    for q in range(D_RNN // GATE_GROUP):
        blk = gate_w[:, q * per_group:(q + 1) * per_group]
        dense = jnp.einsum("gnio,nm->gnimo", blk, eye).reshape(2, GATE_GROUP, GATE_GROUP)
        lo.append(jnp.concatenate([dense[0, :2 * t, :t], dense[1, :2 * t, :t]], axis=1))
        mid.append(jnp.concatenate([dense[0, :, t:2 * t], dense[1, :, t:2 * t]], axis=1))
        hi.append(jnp.concatenate([dense[0, t:, 2 * t:], dense[1, t:, 2 * t:]], axis=1))
    stacks = tuple((0.5 * jnp.stack(w)).astype(_BF16) for w in (lo, mid, hi))
    return stacks, 0.5 * gate_b.reshape(2, D_RNN).astype(_F32)


def kernel(x, emb_ln_g, emb_ln_b, w_in, b_in, rnn_conv_w, rnn_conv_b, rg_gate_w, rg_gate_b, rg_a_param, w_branch_a, conv_w, conv_b, conv_ln_g, conv_ln_b, w_branch_b, w_out, b_out, ln1_g, ln1_b, w_up, b_up, w_down, b_down, ln2_g, ln2_b):
    batch, seq, _ = x.shape
    assert batch == SUBLANES and x.shape[2] == D_MODEL and w_in.shape[0] == 1
    n_rows = batch * seq
    assert n_rows % ROWS == 0 and ROWS % SUB_ROWS == 0 and ROWS % HALO_ROWS == 0
    n_tiles = n_rows // ROWS
    ext_steps = RNN_CONV_LEFT + STEPS + RNN_CONV_RIGHT

    lng, lnb = _row2(emb_ln_g), _row2(emb_ln_b)
    w_in0, b_in0 = w_in[0], b_in[0]
    o_yr, o_xc, o_gl = D_RNN, 2 * D_RNN, 2 * D_RNN + 2 * D_MODEL
    w_r = w_in0[:, :o_yr].astype(_BF16)
    b_r = _row2(b_in0[:o_yr])
    w_yg = jnp.concatenate([w_in0[:, o_yr:o_xc], w_in0[:, o_gl:o_gl + D_MODEL]], axis=1).astype(_BF16)
    b_yg = _row2(jnp.concatenate([b_in0[o_yr:o_xc], b_in0[o_gl:o_gl + D_MODEL]]))
    w_xc = w_in0[:, o_xc:o_gl].astype(_BF16)
    b_xc = _row2(b_in0[o_xc:o_gl])
    w_g1 = w_in0[:, o_gl + D_MODEL:].astype(_BF16)
    b_g1 = _row2(b_in0[o_gl + D_MODEL:])
    wg_f, bg_f = _gate_weights(rg_gate_w[0, 0], rg_gate_b[0, 0])
    wg_b, bg_b = _gate_weights(rg_gate_w[0, 1], rg_gate_b[0, 1])
    decay = -0.5 * RG_C * jax.nn.softplus(-rg_a_param[0].astype(_F32))
    params = pltpu.CompilerParams(dimension_semantics=("arbitrary",), vmem_limit_bytes=VMEM_LIMIT)

    rev = lambda i: n_tiles - 1 - i
    u, h_bwd, hn = pl.pallas_call(
        functools.partial(_rnn_bwd_body, n_tiles),
        grid=(n_tiles,),
        in_specs=[
            pl.BlockSpec(memory_space=pl.ANY),
            _full((1, D_MODEL)), _full((1, D_MODEL)),
            _full((D_MODEL, D_RNN)), _full((1, D_RNN)),
            _full((RNN_CONV_W, D_RNN)), _full((1, D_RNN)),
            *[_full(w.shape) for w in wg_b], _full((2, D_RNN)), _full((1, D_RNN)),
        ],
        out_specs=[pl.BlockSpec((ROWS, D_RNN), lambda i: (rev(i), 0)),
                   pl.BlockSpec((ROWS, D_RNN), lambda i: (rev(i), 0)),
                   pl.BlockSpec((ROWS, D_MODEL), lambda i: (rev(i), 0))],
        out_shape=[jax.ShapeDtypeStruct((n_rows, D_RNN), _F32), jax.ShapeDtypeStruct((n_rows, D_RNN), _F32),
                   jax.ShapeDtypeStruct((n_rows, D_MODEL), _F32)],
        scratch_shapes=[pltpu.VMEM((2, ext_steps, SUBLANES, D_MODEL), _F32), pltpu.SemaphoreType.DMA((2, 3)),
                        pltpu.VMEM((ext_steps * SUBLANES, D_RNN), _F32), pltpu.VMEM((SUBLANES, D_RNN), _F32)],
        compiler_params=params,
        name="rnn_bwd",
    )(x, lng, lnb, w_r, b_r, rnn_conv_w[0].astype(_F32), _row2(rnn_conv_b[0]), *wg_b, bg_b, decay[1:2])

    tile_spec = lambda d: pl.BlockSpec((ROWS, d), lambda i: (i, 0))
    merged_a = pl.pallas_call(
        _rnn_fwd_body,
        grid=(n_tiles,),
        in_specs=[
            tile_spec(D_MODEL), tile_spec(D_RNN), tile_spec(D_RNN),
            _full((D_MODEL, D_RNN + D_MODEL)), _full((1, D_RNN + D_MODEL)),
            *[_full(w.shape) for w in wg_f], _full((2, D_RNN)), _full((1, D_RNN)),
            _full((D_RNN, D_MODEL)),
        ],
        out_specs=tile_spec(D_MODEL),
        out_shape=jax.ShapeDtypeStruct((n_rows, D_MODEL), _F32),
        scratch_shapes=[pltpu.VMEM((ROWS, D_RNN), _BF16), pltpu.VMEM((SUBLANES, D_RNN), _F32)],
        compiler_params=params,
        name="rnn_fwd",
    )(hn, u, h_bwd, w_yg, b_yg, *wg_f, bg_f, decay[0:1], w_branch_a[0].astype(_BF16))

    halo_per_tile = ROWS // HALO_ROWS
    cur_tile = lambda s: jnp.minimum(s, n_tiles - 1)
    prev_tile = lambda s: jnp.maximum(s - 1, 0)
    n_lane_cols = D_MODEL // LANES
    conv_taps = jnp.transpose(
        jnp.repeat(conv_w[0].astype(_F32), SUBLANES, axis=0).reshape(CONV_W * SUBLANES, n_lane_cols, LANES), (1, 0, 2))
    c_rows = ROWS + 2 * HALO_ROWS
    n_col, g_col = 2 * D_MODEL // MXU_COL, D_MODEL // MXU_COL
    assert CONV_CHUNKS % n_col == 0 and CONV_CHUNKS % g_col == 0
    assert ((ROWS + HALO_ROWS) * n_col) % (16 * CONV_CHUNKS) == 0 and (ROWS * g_col) % (16 * CONV_CHUNKS) == 0
    col_tiles = lambda w, n: jnp.transpose(w.reshape(D_MODEL, n, MXU_COL), (1, 0, 2))
    h1 = pl.pallas_call(
        functools.partial(_conv_mix_body, n_tiles),
        grid=(n_tiles + 1,),
        in_specs=[
            pl.BlockSpec((ROWS, D_MODEL), lambda s: (cur_tile(s), 0)),
            pl.BlockSpec((HALO_ROWS, D_MODEL),
                         lambda s: (jnp.minimum((cur_tile(s) + 1) * halo_per_tile, n_rows // HALO_ROWS - 1), 0)),
            pl.BlockSpec((ROWS, D_MODEL), lambda s: (prev_tile(s), 0)),
            pl.BlockSpec((ROWS, D_MODEL), lambda s: (prev_tile(s), 0)),
            _full((n_col, D_MODEL, MXU_COL)), _full((1, 2 * D_MODEL)),
            _full((g_col, D_MODEL, MXU_COL)), _full((1, D_MODEL)),
            _full((n_lane_cols, CONV_W * SUBLANES, LANES)), _full((1, D_MODEL)),
            _full((1, D_MODEL)), _full((1, D_MODEL)),
            _full((D_MODEL, D_MODEL)),
            _full((D_MODEL, D_MODEL)), _full((1, D_MODEL)),
            _full((1, D_MODEL)), _full((1, D_MODEL)),
        ],
        out_specs=pl.BlockSpec((ROWS, D_MODEL), lambda s: (prev_tile(s), 0)),
        out_shape=jax.ShapeDtypeStruct((n_rows, D_MODEL), _F32),
        scratch_shapes=[pltpu.VMEM((ROWS + HALO_ROWS, D_MODEL), _BF16), pltpu.VMEM((ROWS, D_MODEL), _BF16),
                        pltpu.VMEM((n_col, ROWS + HALO_ROWS, MXU_COL), _F32),
                        pltpu.VMEM((g_col, ROWS, MXU_COL), _F32),
                        pltpu.VMEM((n_lane_cols, c_rows, LANES), _F32),
                        pltpu.VMEM((n_lane_cols, c_rows, LANES), _F32),
                        pltpu.VMEM((n_lane_cols, ROWS, LANES), _F32)],
        compiler_params=params,
        name="conv_mix",
    )(hn, hn, hn, merged_a, col_tiles(w_xc, n_col), b_xc, col_tiles(w_g1, g_col), b_g1, conv_taps,
      _row2(conv_b[0]), _row2(conv_ln_g[0]), _row2(conv_ln_b[0]), w_branch_b[0].astype(_BF16),
      w_out[0].astype(_BF16), _row2(b_out[0]), _row2(ln1_g[0]), _row2(ln1_b[0]))

    return pl.pallas_call(
        functools.partial(_mlp_body, n_tiles),
        grid=(n_tiles,),
        in_specs=[
            tile_spec(D_MODEL),
            _full((D_MODEL, D_FF)), _full((1, D_FF)),
            _full((D_FF, D_MODEL)), _full((1, D_MODEL)),
            _full((1, D_MODEL)), _full((1, D_MODEL)),
        ],
        out_specs=pl.BlockSpec(memory_space=pl.ANY),
        out_shape=jax.ShapeDtypeStruct((batch, seq, D_MODEL), _F32),
        scratch_shapes=[pltpu.VMEM((2, STEPS, SUBLANES, D_MODEL), _F32), pltpu.SemaphoreType.DMA((2,))],
        compiler_params=params,
        name="mlp",
    )(h1, w_up[0].astype(_BF16), _row2(b_up[0]), w_down[0].astype(_BF16), _row2(b_down[0]),
      _row2(ln2_g[0]), _row2(ln2_b[0]))
```

```python
import functools

import jax
import jax.numpy as jnp
from jax.experimental import pallas as pl
from jax.experimental.pallas import tpu as pltpu

D_MODEL = 1024
D_RNN = 1536
RNN_BLOCKS = 16
RNN_BLOCK_W = D_RNN // RNN_BLOCKS
GATE_GROUP = 768
RNN_CONV_W = 4
RNN_CONV_LEFT = 2
RNN_CONV_RIGHT = RNN_CONV_W - 1 - RNN_CONV_LEFT
RG_C = 8.0
CONV_W = 31
CONV_HALF = (CONV_W - 1) // 2
D_FF = 4 * D_MODEL
LN_EPS = 1e-5
DEEPNORM_ALPHA = 2.0 ** 0.25

SUBLANES = 8
ROWS = 512
STEPS = ROWS // SUBLANES
SUB_ROWS = 128
HALF_ROWS = ROWS // 2
LANES = 128
MXU_COL = 256
CONV_ROWS = 32
CONV_CHUNKS = ROWS // CONV_ROWS
HALO_ROWS = 128
VMEM_LIMIT = 56 * 1024 * 1024

_BF16 = jnp.bfloat16
_F32 = jnp.float32
_GELU_C = 0.7978845608028654
_F32_TINY = 1.1754943508222875e-38


def _ln(x, g, b):
    mu = jnp.mean(x, axis=-1, keepdims=True)
    xc = x - mu
    var = jnp.mean(xc * xc, axis=-1, keepdims=True)
    return xc * jax.lax.rsqrt(var + LN_EPS) * g + b


def _sigmoid(x):
    return 0.5 * jnp.tanh(0.5 * x) + 0.5


def _gelu_tanh(x):
    hx = 0.5 * x
    t = jnp.tanh(x * ((x * x) * (_GELU_C * 0.044715) + _GELU_C))
    return hx * t + hx


def _dot(a, w):
    return jnp.dot(a.astype(_BF16), w, preferred_element_type=_F32)


def _gate_logits(u, wg_refs):
    wlo_ref, wmid_ref, whi_ref = wg_refs
    ub = u.astype(_BF16)
    r_parts, i_parts = [], []
    for q in range(D_RNN // GATE_GROUP):
        k0 = q * GATE_GROUP
        lo = jnp.dot(ub[:, k0:k0 + 2 * MXU_COL], wlo_ref[q], preferred_element_type=_F32)
        mid = jnp.dot(ub[:, k0:k0 + 3 * MXU_COL], wmid_ref[q], preferred_element_type=_F32)
        hi = jnp.dot(ub[:, k0 + MXU_COL:k0 + 3 * MXU_COL], whi_ref[q], preferred_element_type=_F32)
        for part in (lo, mid, hi):
            r_parts.append(part[:, :MXU_COL])
            i_parts.append(part[:, MXU_COL:])
    return jnp.concatenate(r_parts, axis=1), jnp.concatenate(i_parts, axis=1)


def _gate_terms(u, logits, bg_ref, nhd_ref):
    t_r = jnp.tanh(logits[0] + bg_ref[0:1, :])
    t_i = jnp.tanh(logits[1] + bg_ref[1:2, :])
    nhd = nhd_ref[...]
    log_a = t_r * nhd + nhd
    a = jnp.exp(log_a)
    q2 = jnp.tanh(log_a) * (-1.0 - a * a)
    root = q2 * jax.lax.rsqrt(jnp.maximum(q2, _F32_TINY))
    return a, root * ((0.5 * t_i + 0.5) * u)


def _scan_block(a, bx, h, reverse):
    steps = a.shape[0] // SUBLANES
    outs = [None] * steps
    order = range(steps - 1, -1, -1) if reverse else range(steps)
    for s in order:
        sl = slice(s * SUBLANES, (s + 1) * SUBLANES)
        h = a[sl, :] * h + bx[sl, :]
        outs[s] = h
    return jnp.concatenate(outs, axis=0), h


def _x_tile_copies(x_hbm, xbuf, sems, n_tiles, tile, slot, start):
    t0 = tile * STEPS

    def run(src_start, n_steps, dst_start, sem):
        for b in range(SUBLANES):
            cp = pltpu.make_async_copy(x_hbm.at[b, pl.ds(src_start, n_steps), :],
                                       xbuf.at[slot, pl.ds(dst_start, n_steps), b, :], sem)
            if start:
                cp.start()
            else:
                cp.wait()

    run(t0, STEPS, RNN_CONV_LEFT, sems.at[slot, 0])

    @pl.when(tile > 0)
    def _():
        run(t0 - RNN_CONV_LEFT, RNN_CONV_LEFT, 0, sems.at[slot, 1])

    @pl.when(tile < n_tiles - 1)
    def _():
        run(t0 + STEPS, RNN_CONV_RIGHT, RNN_CONV_LEFT + STEPS, sems.at[slot, 2])


def _rnn_bwd_body(n_tiles, x_hbm, lng_ref, lnb_ref, wr_ref, br_ref, w4_ref, b4_ref,
                  wlo_ref, wmid_ref, whi_ref, bg_ref, decay_ref, wxc_ref, bxc_ref,
                  u_ref, hb_ref, hn_ref, c_ref, xbuf, sems, xr_scr, carry_scr):
    wg_ref = (wlo_ref, wmid_ref, whi_ref)
    i = pl.program_id(0)
    tile = n_tiles - 1 - i
    slot = i % 2
    lead = RNN_CONV_LEFT * SUBLANES
    ext_rows = (RNN_CONV_LEFT + STEPS + RNN_CONV_RIGHT) * SUBLANES

    @pl.when(i == 0)
    def _():
        carry_scr[...] = jnp.zeros_like(carry_scr)
        xbuf[...] = jnp.zeros_like(xbuf)
        _x_tile_copies(x_hbm, xbuf, sems, n_tiles, tile, slot, start=True)

    @pl.when(i + 1 < n_tiles)
    def _():
        _x_tile_copies(x_hbm, xbuf, sems, n_tiles, tile - 1, 1 - slot, start=True)

    _x_tile_copies(x_hbm, xbuf, sems, n_tiles, tile, slot, start=False)

    hn = _ln(xbuf[slot].reshape(ext_rows, D_MODEL), lng_ref[...], lnb_ref[...])
    hn_ref[...] = hn[lead:lead + ROWS, :]
    hnb = hn.astype(_BF16)

    xr = jnp.dot(hnb, wr_ref[...], preferred_element_type=_F32) + br_ref[...]
    xr_scr[0:lead, :] = jnp.where(tile > 0, xr[0:lead, :], 0.0)
    xr_scr[lead:lead + ROWS, :] = xr[lead:lead + ROWS, :]
    xr_scr[lead + ROWS:, :] = jnp.where(tile < n_tiles - 1, xr[lead + ROWS:, :], 0.0)

    def glu_piece(p):
        xc = jnp.dot(hnb[lead:lead + ROWS, :], wxc_ref[p], preferred_element_type=_F32) + bxc_ref[p]
        c = xc[:, :MXU_COL] * _sigmoid(xc[:, MXU_COL:])
        per = MXU_COL // LANES
        for j in range(per):
            c_ref[p * per + j] = c[:, j * LANES:(j + 1) * LANES]

    def conv_and_logits(sb):
        base = sb * SUB_ROWS
        u = jnp.broadcast_to(b4_ref[...], (SUB_ROWS, D_RNN))
        for k in range(RNN_CONV_W):
            u = u + w4_ref[k:k + 1, :] * xr_scr[base + k * SUBLANES:base + k * SUBLANES + SUB_ROWS, :]
        u_ref[base:base + SUB_ROWS, :] = u
        return u, _gate_logits(u, wg_ref)

    n_sub = ROWS // SUB_ROWS
    n_glu = wxc_ref.shape[0]
    h = carry_scr[...]
    ahead = conv_and_logits(n_sub - 1)
    for sb in range(n_sub - 1, -1, -1):
        base = sb * SUB_ROWS
        u, logits = ahead
        if sb > 0:
            ahead = conv_and_logits(sb - 1)
        step = n_sub - 1 - sb
        for p in range(step * n_glu // n_sub, (step + 1) * n_glu // n_sub):
            glu_piece(p)
        a, bx = _gate_terms(u, logits, bg_ref, decay_ref)
        hs, h = _scan_block(a, bx, h, reverse=True)
        hb_ref[base:base + SUB_ROWS, :] = hs
    carry_scr[...] = h


def _rnn_fwd_body(hn_ref, u_ref, hb_ref, wyg_ref, byg_ref, wlo_ref, wmid_ref, whi_ref, bg_ref, decay_ref,
                  wa_ref, ma_ref, z_scr, carry_scr):
    wg_ref = (wlo_ref, wmid_ref, whi_ref)

    @pl.when(pl.program_id(0) == 0)
    def _():
        carry_scr[...] = jnp.zeros_like(carry_scr)

    def logits_of(sb):
        u = u_ref[sb * SUB_ROWS:(sb + 1) * SUB_ROWS, :]
        return u, _gate_logits(u, wg_ref)

    n_sub = ROWS // SUB_ROWS
    ahead = logits_of(0)
    p = _dot(hn_ref[...], wyg_ref[...]) + byg_ref[...]
    h = carry_scr[...]
    for sb in range(n_sub):
        rows = slice(sb * SUB_ROWS, (sb + 1) * SUB_ROWS)
        u, logits = ahead
        if sb + 1 < n_sub:
            ahead = logits_of(sb + 1)
        a, bx = _gate_terms(u, logits, bg_ref, decay_ref)
        hs, h = _scan_block(a, bx, h, reverse=False)
        z_scr[rows, :] = (_gelu_tanh(p[rows, :D_RNN]) * (hs + hb_ref[rows, :])).astype(_BF16)
    carry_scr[...] = h
    y_a = jnp.dot(z_scr[...], wa_ref[...], preferred_element_type=_F32)
    ma_ref[...] = _sigmoid(p[:, D_RNN:]) * y_a


def _conv_mix_body(n_tiles, cp_ref, cm_ref, cn_ref, hn_ref, ma_ref, wg1_ref, bg1_ref,
                   cw_ref, cb_ref, clg_ref, clb_ref, wb_ref, wout_ref, bout_ref, l1g_ref, l1b_ref,
                   h1_ref, c_scr, conv_scr):
    tile = pl.program_id(0)
    n_lane_cols = D_MODEL // LANES

    c_scr[:, 0:HALO_ROWS, :] = jnp.where(tile > 0, cp_ref[...], 0.0)
    c_scr[:, HALO_ROWS:HALO_ROWS + ROWS, :] = cm_ref[...]
    c_scr[:, HALO_ROWS + ROWS:, :] = jnp.where(tile < n_tiles - 1, cn_ref[...], 0.0)

    first = HALO_ROWS - CONV_HALF * SUBLANES
    groups = CONV_ROWS // SUBLANES
    never = tile < 0

    def conv_chunk(i, carry):
        base = pl.multiple_of(i * CONV_ROWS, CONV_ROWS)
        prev = None
        for l in range(n_lane_cols):
            bias = jnp.broadcast_to(cb_ref[:, l * LANES:(l + 1) * LANES], (SUBLANES, LANES))
            acc = [bias if prev is None else jnp.where(never, prev[g], bias) for g in range(groups)]
            for k in range(CONV_W):
                w = cw_ref[l, k * SUBLANES:(k + 1) * SUBLANES, :]
                for g in range(groups):
                    acc[g] = acc[g] + w * c_scr[l, pl.ds(base + first + (k + g) * SUBLANES, SUBLANES), :]
            for g in range(groups):
                conv_scr[l, pl.ds(base + g * SUBLANES, SUBLANES), :] = acc[g]
            prev = acc
        return carry

    jax.lax.fori_loop(0, CONV_CHUNKS, conv_chunk, 0)

    for r0 in range(0, ROWS, HALF_ROWS):
        rows = slice(r0, r0 + HALF_ROWS)
        hn = hn_ref[rows, :]
        conv = jnp.concatenate([conv_scr[l, rows, :] for l in range(n_lane_cols)], axis=1)
        cn = _ln(conv, clg_ref[...], clb_ref[...])
        y_b = _dot(cn * _sigmoid(cn), wb_ref[...])
        g1 = _sigmoid(_dot(hn, wg1_ref[...]) + bg1_ref[...])
        mixed = _dot(ma_ref[rows, :] + g1 * y_b, wout_ref[...]) + bout_ref[...]
        h1_ref[rows, :] = _ln(DEEPNORM_ALPHA * hn + mixed, l1g_ref[...], l1b_ref[...])


def _out_tile_copies(obuf, out_hbm, sems, tile, slot, start):
    for b in range(SUBLANES):
        cp = pltpu.make_async_copy(obuf.at[slot, :, b, :], out_hbm.at[b, pl.ds(tile * STEPS, STEPS), :],
                                   sems.at[slot])
        if start:
            cp.start()
        else:
            cp.wait()


def _mlp_body(n_tiles, h1_ref, wup_ref, bup_ref, wdn_ref, bdn_ref, l2g_ref, l2b_ref, out_hbm, obuf, sems):
    i = pl.program_id(0)
    slot = i % 2

    @pl.when(i >= 2)
    def _():
        _out_tile_copies(obuf, out_hbm, sems, i - 2, slot, start=False)

    for r0 in range(0, ROWS, HALF_ROWS):
        h1 = h1_ref[r0:r0 + HALF_ROWS, :]
        m = jnp.maximum(_dot(h1, wup_ref[...]) + bup_ref[...], 0.0)
        y = _dot(m * m, wdn_ref[...]) + bdn_ref[...]
        res = _ln(DEEPNORM_ALPHA * h1 + y, l2g_ref[...], l2b_ref[...])
        obuf[slot, r0 // SUBLANES:(r0 + HALF_ROWS) // SUBLANES] = res.reshape(HALF_ROWS // SUBLANES, SUBLANES, D_MODEL)
    _out_tile_copies(obuf, out_hbm, sems, i, slot, start=True)

    @pl.when(i == n_tiles - 1)
    def _():
        if n_tiles >= 2:
            _out_tile_copies(obuf, out_hbm, sems, i - 1, 1 - slot, start=False)
        _out_tile_copies(obuf, out_hbm, sems, i, slot, start=False)


def _full(shape):
    return pl.BlockSpec(shape, lambda i: (0,) * len(shape), pipeline_mode=pl.Buffered(1))


def _row2(v):
    return v.reshape(1, -1).astype(_F32)


def _gate_weights(gate_w, gate_b):
    t = MXU_COL
    per_group = GATE_GROUP // RNN_BLOCK_W
    assert GATE_GROUP == 3 * t and per_group * RNN_BLOCK_W == GATE_GROUP
    assert (-(-t // RNN_BLOCK_W)) * RNN_BLOCK_W <= 2 * t
    eye = jnp.eye(per_group, dtype=gate_w.dtype)
    lo, mid, hi = [], [], []
    for q in range(D_RNN // GATE_GROUP):
        blk = gate_w[:, q * per_group:(q + 1) * per_group]
        dense = jnp.einsum("gnio,nm->gnimo", blk, eye).reshape(2, GATE_GROUP, GATE_GROUP)
        lo.append(jnp.concatenate([dense[0, :2 * t, :t], dense[1, :2 * t, :t]], axis=1))
        mid.append(jnp.concatenate([dense[0, :, t:2 * t], dense[1, :, t:2 * t]], axis=1))
        hi.append(jnp.concatenate([dense[0, t:, 2 * t:], dense[1, t:, 2 * t:]], axis=1))
    stacks = tuple((0.5 * jnp.stack(w)).astype(_BF16) for w in (lo, mid, hi))
    return stacks, 0.5 * gate_b.reshape(2, D_RNN).astype(_F32)


def kernel(x, emb_ln_g, emb_ln_b, w_in, b_in, rnn_conv_w, rnn_conv_b, rg_gate_w, rg_gate_b, rg_a_param, w_branch_a, conv_w, conv_b, conv_ln_g, conv_ln_b, w_branch_b, w_out, b_out, ln1_g, ln1_b, w_up, b_up, w_down, b_down, ln2_g, ln2_b):
    batch, seq, _ = x.shape
    assert batch == SUBLANES and x.shape[2] == D_MODEL and w_in.shape[0] == 1
    n_rows = batch * seq
    assert n_rows % ROWS == 0 and ROWS % SUB_ROWS == 0 and ROWS % HALO_ROWS == 0
    n_tiles = n_rows // ROWS
    ext_steps = RNN_CONV_LEFT + STEPS + RNN_CONV_RIGHT

    lng, lnb = _row2(emb_ln_g), _row2(emb_ln_b)
    w_in0, b_in0 = w_in[0], b_in[0]
    o_yr, o_xc, o_gl = D_RNN, 2 * D_RNN, 2 * D_RNN + 2 * D_MODEL
    w_r = w_in0[:, :o_yr].astype(_BF16)
    b_r = _row2(b_in0[:o_yr])
    w_yg = jnp.concatenate([w_in0[:, o_yr:o_xc], w_in0[:, o_gl:o_gl + D_MODEL]], axis=1).astype(_BF16)
    b_yg = _row2(jnp.concatenate([b_in0[o_yr:o_xc], b_in0[o_gl:o_gl + D_MODEL]]))
    w_xc = w_in0[:, o_xc:o_gl].astype(_BF16)
    b_xc = _row2(b_in0[o_xc:o_gl])
    w_g1 = w_in0[:, o_gl + D_MODEL:].astype(_BF16)
    b_g1 = _row2(b_in0[o_gl + D_MODEL:])
    wg_f, bg_f = _gate_weights(rg_gate_w[0, 0], rg_gate_b[0, 0])
    wg_b, bg_b = _gate_weights(rg_gate_w[0, 1], rg_gate_b[0, 1])
    decay = -0.5 * RG_C * jax.nn.softplus(-rg_a_param[0].astype(_F32))
    params = pltpu.CompilerParams(dimension_semantics=("arbitrary",), vmem_limit_bytes=VMEM_LIMIT)

    rev = lambda i: n_tiles - 1 - i
    n_lane_cols = D_MODEL // LANES
    n_glu = D_MODEL // MXU_COL
    pair = lambda a, p: jnp.concatenate([a[..., p * MXU_COL:(p + 1) * MXU_COL],
                                         a[..., D_MODEL + p * MXU_COL:D_MODEL + (p + 1) * MXU_COL]], axis=-1)
    w_glu = jnp.stack([pair(w_xc, p) for p in range(n_glu)])
    b_glu = jnp.stack([pair(b_xc, p) for p in range(n_glu)])
    u, h_bwd, hn, c_glu = pl.pallas_call(
        functools.partial(_rnn_bwd_body, n_tiles),
        grid=(n_tiles,),
        in_specs=[
            pl.BlockSpec(memory_space=pl.ANY),
            _full((1, D_MODEL)), _full((1, D_MODEL)),
            _full((D_MODEL, D_RNN)), _full((1, D_RNN)),
            _full((RNN_CONV_W, D_RNN)), _full((1, D_RNN)),
            *[_full(w.shape) for w in wg_b], _full((2, D_RNN)), _full((1, D_RNN)),
            _full(w_glu.shape), _full(b_glu.shape),
        ],
        out_specs=[pl.BlockSpec((ROWS, D_RNN), lambda i: (rev(i), 0)),
                   pl.BlockSpec((ROWS, D_RNN), lambda i: (rev(i), 0)),
                   pl.BlockSpec((ROWS, D_MODEL), lambda i: (rev(i), 0)),
                   pl.BlockSpec((n_lane_cols, ROWS, LANES), lambda i: (0, rev(i), 0))],
        out_shape=[jax.ShapeDtypeStruct((n_rows, D_RNN), _F32), jax.ShapeDtypeStruct((n_rows, D_RNN), _F32),
                   jax.ShapeDtypeStruct((n_rows, D_MODEL), _F32),
                   jax.ShapeDtypeStruct((n_lane_cols, n_rows, LANES), _F32)],
        scratch_shapes=[pltpu.VMEM((2, ext_steps, SUBLANES, D_MODEL), _F32), pltpu.SemaphoreType.DMA((2, 3)),
                        pltpu.VMEM((ext_steps * SUBLANES, D_RNN), _F32), pltpu.VMEM((SUBLANES, D_RNN), _F32)],
        compiler_params=params,
        name="rnn_bwd",
    )(x, lng, lnb, w_r, b_r, rnn_conv_w[0].astype(_F32), _row2(rnn_conv_b[0]), *wg_b, bg_b, decay[1:2],
      w_glu, b_glu)

    tile_spec = lambda d: pl.BlockSpec((ROWS, d), lambda i: (i, 0))
    merged_a = pl.pallas_call(
        _rnn_fwd_body,
        grid=(n_tiles,),
        in_specs=[
            tile_spec(D_MODEL), tile_spec(D_RNN), tile_spec(D_RNN),
            _full((D_MODEL, D_RNN + D_MODEL)), _full((1, D_RNN + D_MODEL)),
            *[_full(w.shape) for w in wg_f], _full((2, D_RNN)), _full((1, D_RNN)),
            _full((D_RNN, D_MODEL)),
        ],
        out_specs=tile_spec(D_MODEL),
        out_shape=jax.ShapeDtypeStruct((n_rows, D_MODEL), _F32),
        scratch_shapes=[pltpu.VMEM((ROWS, D_RNN), _BF16), pltpu.VMEM((SUBLANES, D_RNN), _F32)],
        compiler_params=params,
        name="rnn_fwd",
    )(hn, u, h_bwd, w_yg, b_yg, *wg_f, bg_f, decay[0:1], w_branch_a[0].astype(_BF16))

    halo_per_tile = ROWS // HALO_ROWS
    n_halo_blocks = n_rows // HALO_ROWS
    conv_taps = jnp.transpose(
        jnp.repeat(conv_w[0].astype(_F32), SUBLANES, axis=0).reshape(CONV_W * SUBLANES, n_lane_cols, LANES), (1, 0, 2))
    h1 = pl.pallas_call(
        functools.partial(_conv_mix_body, n_tiles),
        grid=(n_tiles,),
        in_specs=[
            pl.BlockSpec((n_lane_cols, HALO_ROWS, LANES),
                         lambda i: (0, jnp.maximum(i * halo_per_tile - 1, 0), 0)),
            pl.BlockSpec((n_lane_cols, ROWS, LANES), lambda i: (0, i, 0)),
            pl.BlockSpec((n_lane_cols, HALO_ROWS, LANES),
                         lambda i: (0, jnp.minimum((i + 1) * halo_per_tile, n_halo_blocks - 1), 0)),
            tile_spec(D_MODEL), tile_spec(D_MODEL),
            _full((D_MODEL, D_MODEL)), _full((1, D_MODEL)),
            _full((n_lane_cols, CONV_W * SUBLANES, LANES)), _full((1, D_MODEL)),
            _full((1, D_MODEL)), _full((1, D_MODEL)),
            _full((D_MODEL, D_MODEL)),
            _full((D_MODEL, D_MODEL)), _full((1, D_MODEL)),
            _full((1, D_MODEL)), _full((1, D_MODEL)),
        ],
        out_specs=tile_spec(D_MODEL),
        out_shape=jax.ShapeDtypeStruct((n_rows, D_MODEL), _F32),
        scratch_shapes=[pltpu.VMEM((n_lane_cols, ROWS + 2 * HALO_ROWS, LANES), _F32),
                        pltpu.VMEM((n_lane_cols, ROWS, LANES), _F32)],
        compiler_params=params,
        name="conv_mix",
    )(c_glu, c_glu, c_glu, hn, merged_a, w_g1, b_g1, conv_taps,
      _row2(conv_b[0]), _row2(conv_ln_g[0]), _row2(conv_ln_b[0]), w_branch_b[0].astype(_BF16),
      w_out[0].astype(_BF16), _row2(b_out[0]), _row2(ln1_g[0]), _row2(ln1_b[0]))

    return pl.pallas_call(
        functools.partial(_mlp_body, n_tiles),
        grid=(n_tiles,),
        in_specs=[
            tile_spec(D_MODEL),
            _full((D_MODEL, D_FF)), _full((1, D_FF)),
            _full((D_FF, D_MODEL)), _full((1, D_MODEL)),
            _full((1, D_MODEL)), _full((1, D_MODEL)),
        ],
        out_specs=pl.BlockSpec(memory_space=pl.ANY),
        out_shape=jax.ShapeDtypeStruct((batch, seq, D_MODEL), _F32),
        scratch_shapes=[pltpu.VMEM((2, STEPS, SUBLANES, D_MODEL), _F32), pltpu.SemaphoreType.DMA((2,))],
        compiler_params=params,
        name="mlp",
    )(h1, w_up[0].astype(_BF16), _row2(b_up[0]), w_down[0].astype(_BF16), _row2(b_down[0]),
      _row2(ln2_g[0]), _row2(ln2_b[0]))
```

```python
import functools

import jax
import jax.numpy as jnp
from jax.experimental import pallas as pl
from jax.experimental.pallas import tpu as pltpu

D_MODEL = 1024
D_RNN = 1536
RNN_BLOCKS = 16
RNN_BLOCK_W = D_RNN // RNN_BLOCKS
GATE_GROUP = 768
RNN_CONV_W = 4
RNN_CONV_LEFT = 2
RNN_CONV_RIGHT = RNN_CONV_W - 1 - RNN_CONV_LEFT
RG_C = 8.0
CONV_W = 31
CONV_HALF = (CONV_W - 1) // 2
D_FF = 4 * D_MODEL
LN_EPS = 1e-5
DEEPNORM_ALPHA = 2.0 ** 0.25

SUBLANES = 8
ROWS = 512
STEPS = ROWS // SUBLANES
SUB_ROWS = 128
HALF_ROWS = ROWS // 2
LANES = 128
MXU_COL = 256
CONV_ROWS = 32
HALO_ROWS = 128
VMEM_LIMIT = 56 * 1024 * 1024

_BF16 = jnp.bfloat16
_F32 = jnp.float32
_GELU_C = 0.7978845608028654
_F32_TINY = 1.1754943508222875e-38


def _ln(x, g, b):
    mu = jnp.mean(x, axis=-1, keepdims=True)
    xc = x - mu
    var = jnp.mean(xc * xc, axis=-1, keepdims=True)
    return xc * jax.lax.rsqrt(var + LN_EPS) * g + b


def _sigmoid(x):
    return 0.5 * jnp.tanh(0.5 * x) + 0.5


def _gelu_tanh(x):
    hx = 0.5 * x
    t = jnp.tanh(x * ((x * x) * (_GELU_C * 0.044715) + _GELU_C))
    return hx * t + hx


def _dot(a, w):
    return jnp.dot(a.astype(_BF16), w, preferred_element_type=_F32)


def _gate_logits(u, wg_refs):
    wlo_ref, wmid_ref, whi_ref = wg_refs
    ub = u.astype(_BF16)
    r_parts, i_parts = [], []
    for q in range(D_RNN // GATE_GROUP):
        k0 = q * GATE_GROUP
        lo = jnp.dot(ub[:, k0:k0 + 2 * MXU_COL], wlo_ref[q], preferred_element_type=_F32)
        mid = jnp.dot(ub[:, k0:k0 + 3 * MXU_COL], wmid_ref[q], preferred_element_type=_F32)
        hi = jnp.dot(ub[:, k0 + MXU_COL:k0 + 3 * MXU_COL], whi_ref[q], preferred_element_type=_F32)
        for part in (lo, mid, hi):
            r_parts.append(part[:, :MXU_COL])
            i_parts.append(part[:, MXU_COL:])
    return jnp.concatenate(r_parts, axis=1), jnp.concatenate(i_parts, axis=1)


def _gate_terms(u, logits, bg_ref, nhd_ref):
    t_r = jnp.tanh(logits[0] + bg_ref[0:1, :])
    t_i = jnp.tanh(logits[1] + bg_ref[1:2, :])
    nhd = nhd_ref[...]
    log_a = t_r * nhd + nhd
    a = jnp.exp(log_a)
    q2 = jnp.tanh(log_a) * (-1.0 - a * a)
    root = q2 * jax.lax.rsqrt(jnp.maximum(q2, _F32_TINY))
    return a, root * ((0.5 * t_i + 0.5) * u)


def _scan_block(a, bx, h, reverse):
    steps = a.shape[0] // SUBLANES
    outs = [None] * steps
    order = range(steps - 1, -1, -1) if reverse else range(steps)
    for s in order:
        sl = slice(s * SUBLANES, (s + 1) * SUBLANES)
        h = a[sl, :] * h + bx[sl, :]
        outs[s] = h
    return jnp.concatenate(outs, axis=0), h


def _x_tile_copies(x_hbm, xbuf, sems, n_tiles, tile, slot, start):
    t0 = tile * STEPS

    def run(src_start, n_steps, dst_start, sem):
        for b in range(SUBLANES):
            cp = pltpu.make_async_copy(x_hbm.at[b, pl.ds(src_start, n_steps), :],
                                       xbuf.at[slot, pl.ds(dst_start, n_steps), b, :], sem)
            if start:
                cp.start()
            else:
                cp.wait()

    run(t0, STEPS, RNN_CONV_LEFT, sems.at[slot, 0])

    @pl.when(tile > 0)
    def _():
        run(t0 - RNN_CONV_LEFT, RNN_CONV_LEFT, 0, sems.at[slot, 1])

    @pl.when(tile < n_tiles - 1)
    def _():
        run(t0 + STEPS, RNN_CONV_RIGHT, RNN_CONV_LEFT + STEPS, sems.at[slot, 2])


def _rnn_bwd_body(n_tiles, x_hbm, lng_ref, lnb_ref, wr_ref, br_ref, w4_ref, b4_ref,
                  wlo_ref, wmid_ref, whi_ref, bg_ref, decay_ref, wxc_ref, bxc_ref,
                  u_ref, hb_ref, hn_ref, c_ref, xbuf, sems, xr_scr, carry_scr):
    wg_ref = (wlo_ref, wmid_ref, whi_ref)
    i = pl.program_id(0)
    tile = n_tiles - 1 - i
    slot = i % 2
    lead = RNN_CONV_LEFT * SUBLANES
    ext_rows = (RNN_CONV_LEFT + STEPS + RNN_CONV_RIGHT) * SUBLANES

    @pl.when(i == 0)
    def _():
        carry_scr[...] = jnp.zeros_like(carry_scr)
        xbuf[...] = jnp.zeros_like(xbuf)
        _x_tile_copies(x_hbm, xbuf, sems, n_tiles, tile, slot, start=True)

    @pl.when(i + 1 < n_tiles)
    def _():
        _x_tile_copies(x_hbm, xbuf, sems, n_tiles, tile - 1, 1 - slot, start=True)

    _x_tile_copies(x_hbm, xbuf, sems, n_tiles, tile, slot, start=False)

    hn = _ln(xbuf[slot].reshape(ext_rows, D_MODEL), lng_ref[...], lnb_ref[...])
    hn_ref[...] = hn[lead:lead + ROWS, :]
    hnb = hn.astype(_BF16)

    xr = jnp.dot(hnb, wr_ref[...], preferred_element_type=_F32) + br_ref[...]
    xr_scr[0:lead, :] = jnp.where(tile > 0, xr[0:lead, :], 0.0)
    xr_scr[lead:lead + ROWS, :] = xr[lead:lead + ROWS, :]
    xr_scr[lead + ROWS:, :] = jnp.where(tile < n_tiles - 1, xr[lead + ROWS:, :], 0.0)

    def glu_piece(p):
        xc = jnp.dot(hnb[lead:lead + ROWS, :], wxc_ref[p], preferred_element_type=_F32) + bxc_ref[p]
        c = xc[:, :MXU_COL] * _sigmoid(xc[:, MXU_COL:])
        per = MXU_COL // LANES
        for j in range(per):
            c_ref[p * per + j] = c[:, j * LANES:(j + 1) * LANES]

    def conv_and_logits(sb):
        base = sb * SUB_ROWS
        u = jnp.broadcast_to(b4_ref[...], (SUB_ROWS, D_RNN))
        for k in range(RNN_CONV_W):
            u = u + w4_ref[k:k + 1, :] * xr_scr[base + k * SUBLANES:base + k * SUBLANES + SUB_ROWS, :]
        u_ref[base:base + SUB_ROWS, :] = u
        return u, _gate_logits(u, wg_ref)

    n_sub = ROWS // SUB_ROWS
    n_glu = wxc_ref.shape[0]
    h = carry_scr[...]
    ahead = conv_and_logits(n_sub - 1)
    for sb in range(n_sub - 1, -1, -1):
        base = sb * SUB_ROWS
        u, logits = ahead
        if sb > 0:
            ahead = conv_and_logits(sb - 1)
        step = n_sub - 1 - sb
        for p in range(step * n_glu // n_sub, (step + 1) * n_glu // n_sub):
            glu_piece(p)
        a, bx = _gate_terms(u, logits, bg_ref, decay_ref)
        hs, h = _scan_block(a, bx, h, reverse=True)
        hb_ref[base:base + SUB_ROWS, :] = hs
    carry_scr[...] = h


def _rnn_fwd_body(hn_ref, u_ref, hb_ref, wyg_ref, byg_ref, wlo_ref, wmid_ref, whi_ref, bg_ref, decay_ref,
                  wa_ref, ma_ref, z_scr, carry_scr):
    wg_ref = (wlo_ref, wmid_ref, whi_ref)

    @pl.when(pl.program_id(0) == 0)
    def _():
        carry_scr[...] = jnp.zeros_like(carry_scr)

    def logits_of(sb):
        u = u_ref[sb * SUB_ROWS:(sb + 1) * SUB_ROWS, :]
        return u, _gate_logits(u, wg_ref)

    n_sub = ROWS // SUB_ROWS
    ahead = logits_of(0)
    p = _dot(hn_ref[...], wyg_ref[...]) + byg_ref[...]
    h = carry_scr[...]
    for sb in range(n_sub):
        rows = slice(sb * SUB_ROWS, (sb + 1) * SUB_ROWS)
        u, logits = ahead
        if sb + 1 < n_sub:
            ahead = logits_of(sb + 1)
        a, bx = _gate_terms(u, logits, bg_ref, decay_ref)
        hs, h = _scan_block(a, bx, h, reverse=False)
        z_scr[rows, :] = (_gelu_tanh(p[rows, :D_RNN]) * (hs + hb_ref[rows, :])).astype(_BF16)
    carry_scr[...] = h
    y_a = jnp.dot(z_scr[...], wa_ref[...], preferred_element_type=_F32)
    ma_ref[...] = _sigmoid(p[:, D_RNN:]) * y_a


def _out_tile_copies(obuf, out_hbm, sem, tile, start):
    for b in range(SUBLANES):
        cp = pltpu.make_async_copy(obuf.at[:, b, :], out_hbm.at[b, pl.ds(tile * STEPS, STEPS), :], sem.at[0])
        if start:
            cp.start()
        else:
            cp.wait()


def _conv_mlp_body(n_tiles, cp_ref, cm_ref, cn_ref, hn_ref, ma_ref, cw_ref, cb_ref, clg_ref, clb_ref,
                   wb_ref, wg1_ref, bg1_ref, wout_ref, bout_ref, l1g_ref, l1b_ref,
                   wup_ref, bup_ref, wdn_ref, bdn_ref, l2g_ref, l2b_ref,
                   out_hbm, c_scr, conv_scr, cnb_scr, obuf, sem):
    s = pl.program_id(0)
    conv_tile = jnp.minimum(s, n_tiles - 1)
    n_lane_cols = D_MODEL // LANES

    @pl.when(s == 0)
    def _():
        cnb_scr[...] = jnp.zeros_like(cnb_scr)

    c_scr[:, 0:HALO_ROWS, :] = jnp.where(conv_tile > 0, cp_ref[...], 0.0)
    c_scr[:, HALO_ROWS:HALO_ROWS + ROWS, :] = cm_ref[...]
    c_scr[:, HALO_ROWS + ROWS:, :] = jnp.where(conv_tile < n_tiles - 1, cn_ref[...], 0.0)

    first = HALO_ROWS - CONV_HALF * SUBLANES
    groups = CONV_ROWS // SUBLANES
    never = s < 0
    n_conv_tiles = n_lane_cols * (ROWS // CONV_ROWS)
    tiles_per_half = n_conv_tiles // (ROWS // HALF_ROWS)
    conv_state = {"prev": None, "next": 0, "credit": 0.0}
    pushes_per_step = (ROWS // 16) * (3 * (D_MODEL // MXU_COL) ** 2 + 2 * (D_MODEL // MXU_COL) * (D_FF // MXU_COL))

    def conv_tiles(count):
        for t in range(conv_state["next"], min(conv_state["next"] + count, n_conv_tiles)):
            r0, l = (t // n_lane_cols) * CONV_ROWS, t % n_lane_cols
            prev = conv_state["prev"]
            bias = jnp.broadcast_to(cb_ref[:, l * LANES:(l + 1) * LANES], (SUBLANES, LANES))
            acc = [bias if prev is None else jnp.where(never, prev[g], bias) for g in range(groups)]
            for k in range(CONV_W):
                w = cw_ref[l, k * SUBLANES:(k + 1) * SUBLANES, :]
                for g in range(groups):
                    row = r0 + first + (k + g) * SUBLANES
                    acc[g] = acc[g] + w * c_scr[l, row:row + SUBLANES, :]
            for g in range(groups):
                conv_scr[l, r0 + g * SUBLANES:r0 + (g + 1) * SUBLANES, :] = acc[g]
            conv_state["prev"] = acc
            if (t + 1) % tiles_per_half == 0:
                rows = slice((t + 1 - tiles_per_half) // n_lane_cols * CONV_ROWS, (t + 1) // n_lane_cols * CONV_ROWS)
                conv = jnp.concatenate([conv_scr[j, rows, :] for j in range(n_lane_cols)], axis=1)
                cn = _ln(conv, clg_ref[...], clb_ref[...])
                cnb_scr[rows, :] = (cn * _sigmoid(cn)).astype(_BF16)
        conv_state["next"] = min(conv_state["next"] + count, n_conv_tiles)

    def pieces(lhs, w_ref):
        k_tiles = w_ref.shape[0] // MXU_COL
        out = []
        for j in range(w_ref.shape[1] // MXU_COL):
            piece = jnp.dot(lhs, w_ref[:, j * MXU_COL:(j + 1) * MXU_COL], preferred_element_type=_F32)
            conv_state["credit"] += (lhs.shape[0] // 16) * k_tiles * n_conv_tiles / pushes_per_step
            whole = int(conv_state["credit"])
            conv_state["credit"] -= whole
            conv_tiles(whole)
            if whole and conv_state["prev"] is not None:
                corner = jnp.where(never, conv_state["prev"][0], piece[0:SUBLANES, 0:LANES])
                top = jnp.concatenate([corner, piece[0:SUBLANES, LANES:]], axis=1)
                piece = jnp.concatenate([top, piece[SUBLANES:, :]], axis=0)
            out.append(piece)
        return jnp.concatenate(out, axis=1)

    for r0 in range(0, ROWS, HALF_ROWS):
        rows = slice(r0, r0 + HALF_ROWS)
        hn = hn_ref[rows, :]
        y_b = pieces(cnb_scr[rows, :], wb_ref)
        g1 = _sigmoid(pieces(hn.astype(_BF16), wg1_ref) + bg1_ref[...])
        mixed = pieces((ma_ref[rows, :] + g1 * y_b).astype(_BF16), wout_ref) + bout_ref[...]
        h1 = _ln(DEEPNORM_ALPHA * hn + mixed, l1g_ref[...], l1b_ref[...])
        m = jnp.maximum(pieces(h1.astype(_BF16), wup_ref) + bup_ref[...], 0.0)
        y = pieces((m * m).astype(_BF16), wdn_ref) + bdn_ref[...]
        res = _ln(DEEPNORM_ALPHA * h1 + y, l2g_ref[...], l2b_ref[...])
        if r0 == 0:
            @pl.when(s >= 2)
            def _():
                _out_tile_copies(obuf, out_hbm, sem, s - 2, start=False)
        obuf[r0 // SUBLANES:(r0 + HALF_ROWS) // SUBLANES] = res.reshape(HALF_ROWS // SUBLANES, SUBLANES, D_MODEL)
    conv_tiles(n_conv_tiles)
    assert conv_state["next"] == n_conv_tiles

    @pl.when(s >= 1)
    def _():
        _out_tile_copies(obuf, out_hbm, sem, s - 1, start=True)

    @pl.when(s == n_tiles)
    def _():
        _out_tile_copies(obuf, out_hbm, sem, s - 1, start=False)


def _full(shape):
    return pl.BlockSpec(shape, lambda i: (0,) * len(shape), pipeline_mode=pl.Buffered(1))


def _row2(v):
    return v.reshape(1, -1).astype(_F32)


def _gate_weights(gate_w, gate_b):
    t = MXU_COL
    per_group = GATE_GROUP // RNN_BLOCK_W
    assert GATE_GROUP == 3 * t and per_group * RNN_BLOCK_W == GATE_GROUP
    assert (-(-t // RNN_BLOCK_W)) * RNN_BLOCK_W <= 2 * t
    eye = jnp.eye(per_group, dtype=gate_w.dtype)
    lo, mid, hi = [], [], []
    for q in range(D_RNN // GATE_GROUP):
        blk = gate_w[:, q * per_group:(q + 1) * per_group]
        dense = jnp.einsum("gnio,nm->gnimo", blk, eye).reshape(2, GATE_GROUP, GATE_GROUP)
        lo.append(jnp.concatenate([dense[0, :2 * t, :t], dense[1, :2 * t, :t]], axis=1))
        mid.append(jnp.concatenate([dense[0, :, t:2 * t], dense[1, :, t:2 * t]], axis=1))
        hi.append(jnp.concatenate([dense[0, t:, 2 * t:], dense[1, t:, 2 * t:]], axis=1))
    stacks = tuple((0.5 * jnp.stack(w)).astype(_BF16) for w in (lo, mid, hi))
    return stacks, 0.5 * gate_b.reshape(2, D_RNN).astype(_F32)


def kernel(x, emb_ln_g, emb_ln_b, w_in, b_in, rnn_conv_w, rnn_conv_b, rg_gate_w, rg_gate_b, rg_a_param, w_branch_a, conv_w, conv_b, conv_ln_g, conv_ln_b, w_branch_b, w_out, b_out, ln1_g, ln1_b, w_up, b_up, w_down, b_down, ln2_g, ln2_b):
    batch, seq, _ = x.shape
    assert batch == SUBLANES and x.shape[2] == D_MODEL and w_in.shape[0] == 1
    n_rows = batch * seq
    assert n_rows % ROWS == 0 and ROWS % SUB_ROWS == 0 and ROWS % HALO_ROWS == 0
    n_tiles = n_rows // ROWS
    ext_steps = RNN_CONV_LEFT + STEPS + RNN_CONV_RIGHT

    lng, lnb = _row2(emb_ln_g), _row2(emb_ln_b)
    w_in0, b_in0 = w_in[0], b_in[0]
    o_yr, o_xc, o_gl = D_RNN, 2 * D_RNN, 2 * D_RNN + 2 * D_MODEL
    w_r = w_in0[:, :o_yr].astype(_BF16)
    b_r = _row2(b_in0[:o_yr])
    w_yg = jnp.concatenate([w_in0[:, o_yr:o_xc], w_in0[:, o_gl:o_gl + D_MODEL]], axis=1).astype(_BF16)
    b_yg = _row2(jnp.concatenate([b_in0[o_yr:o_xc], b_in0[o_gl:o_gl + D_MODEL]]))
    w_xc = w_in0[:, o_xc:o_gl].astype(_BF16)
    b_xc = _row2(b_in0[o_xc:o_gl])
    w_g1 = w_in0[:, o_gl + D_MODEL:].astype(_BF16)
    b_g1 = _row2(b_in0[o_gl + D_MODEL:])
    wg_f, bg_f = _gate_weights(rg_gate_w[0, 0], rg_gate_b[0, 0])
    wg_b, bg_b = _gate_weights(rg_gate_w[0, 1], rg_gate_b[0, 1])
    decay = -0.5 * RG_C * jax.nn.softplus(-rg_a_param[0].astype(_F32))
    params = pltpu.CompilerParams(dimension_semantics=("arbitrary",), vmem_limit_bytes=VMEM_LIMIT)

    rev = lambda i: n_tiles - 1 - i
    n_lane_cols = D_MODEL // LANES
    n_glu = D_MODEL // MXU_COL
    pair = lambda a, p: jnp.concatenate([a[..., p * MXU_COL:(p + 1) * MXU_COL],
                                         a[..., D_MODEL + p * MXU_COL:D_MODEL + (p + 1) * MXU_COL]], axis=-1)
    w_glu = jnp.stack([pair(w_xc, p) for p in range(n_glu)])
    b_glu = jnp.stack([pair(b_xc, p) for p in range(n_glu)])
    u, h_bwd, hn, c_glu = pl.pallas_call(
        functools.partial(_rnn_bwd_body, n_tiles),
        grid=(n_tiles,),
        in_specs=[
            pl.BlockSpec(memory_space=pl.ANY),
            _full((1, D_MODEL)), _full((1, D_MODEL)),
            _full((D_MODEL, D_RNN)), _full((1, D_RNN)),
            _full((RNN_CONV_W, D_RNN)), _full((1, D_RNN)),
            *[_full(w.shape) for w in wg_b], _full((2, D_RNN)), _full((1, D_RNN)),
            _full(w_glu.shape), _full(b_glu.shape),
        ],
        out_specs=[pl.BlockSpec((ROWS, D_RNN), lambda i: (rev(i), 0)),
                   pl.BlockSpec((ROWS, D_RNN), lambda i: (rev(i), 0)),
                   pl.BlockSpec((ROWS, D_MODEL), lambda i: (rev(i), 0)),
                   pl.BlockSpec((n_lane_cols, ROWS, LANES), lambda i: (0, rev(i), 0))],
        out_shape=[jax.ShapeDtypeStruct((n_rows, D_RNN), _F32), jax.ShapeDtypeStruct((n_rows, D_RNN), _F32),
                   jax.ShapeDtypeStruct((n_rows, D_MODEL), _F32),
                   jax.ShapeDtypeStruct((n_lane_cols, n_rows, LANES), _F32)],
        scratch_shapes=[pltpu.VMEM((2, ext_steps, SUBLANES, D_MODEL), _F32), pltpu.SemaphoreType.DMA((2, 3)),
                        pltpu.VMEM((ext_steps * SUBLANES, D_RNN), _F32), pltpu.VMEM((SUBLANES, D_RNN), _F32)],
        compiler_params=params,
        name="rnn_bwd",
    )(x, lng, lnb, w_r, b_r, rnn_conv_w[0].astype(_F32), _row2(rnn_conv_b[0]), *wg_b, bg_b, decay[1:2],
      w_glu, b_glu)

    tile_spec = lambda d: pl.BlockSpec((ROWS, d), lambda i: (i, 0))
    merged_a = pl.pallas_call(
        _rnn_fwd_body,
        grid=(n_tiles,),
        in_specs=[
            tile_spec(D_MODEL), tile_spec(D_RNN), tile_spec(D_RNN),
            _full((D_MODEL, D_RNN + D_MODEL)), _full((1, D_RNN + D_MODEL)),
            *[_full(w.shape) for w in wg_f], _full((2, D_RNN)), _full((1, D_RNN)),
            _full((D_RNN, D_MODEL)),
        ],
        out_specs=tile_spec(D_MODEL),
        out_shape=jax.ShapeDtypeStruct((n_rows, D_MODEL), _F32),
        scratch_shapes=[pltpu.VMEM((ROWS, D_RNN), _BF16), pltpu.VMEM((SUBLANES, D_RNN), _F32)],
        compiler_params=params,
        name="rnn_fwd",
    )(hn, u, h_bwd, w_yg, b_yg, *wg_f, bg_f, decay[0:1], w_branch_a[0].astype(_BF16))

    halo_per_tile = ROWS // HALO_ROWS
    n_halo_blocks = n_rows // HALO_ROWS
    conv_tile = lambda s: jnp.minimum(s, n_tiles - 1)
    prev_tile_spec = pl.BlockSpec((ROWS, D_MODEL), lambda s: (jnp.maximum(s - 1, 0), 0))
    conv_taps = jnp.transpose(
        jnp.repeat(conv_w[0].astype(_F32), SUBLANES, axis=0).reshape(CONV_W * SUBLANES, n_lane_cols, LANES), (1, 0, 2))
    return pl.pallas_call(
        functools.partial(_conv_mlp_body, n_tiles),
        grid=(n_tiles + 1,),
        in_specs=[
            pl.BlockSpec((n_lane_cols, HALO_ROWS, LANES),
                         lambda s: (0, jnp.maximum(conv_tile(s) * halo_per_tile - 1, 0), 0)),
            pl.BlockSpec((n_lane_cols, ROWS, LANES), lambda s: (0, conv_tile(s), 0)),
            pl.BlockSpec((n_lane_cols, HALO_ROWS, LANES),
                         lambda s: (0, jnp.minimum((conv_tile(s) + 1) * halo_per_tile, n_halo_blocks - 1), 0)),
            prev_tile_spec, prev_tile_spec,
            _full((n_lane_cols, CONV_W * SUBLANES, LANES)), _full((1, D_MODEL)),
            _full((1, D_MODEL)), _full((1, D_MODEL)),
            _full((D_MODEL, D_MODEL)),
            _full((D_MODEL, D_MODEL)), _full((1, D_MODEL)),
            _full((D_MODEL, D_MODEL)), _full((1, D_MODEL)),
            _full((1, D_MODEL)), _full((1, D_MODEL)),
            _full((D_MODEL, D_FF)), _full((1, D_FF)),
            _full((D_FF, D_MODEL)), _full((1, D_MODEL)),
            _full((1, D_MODEL)), _full((1, D_MODEL)),
        ],
        out_specs=pl.BlockSpec(memory_space=pl.ANY),
        out_shape=jax.ShapeDtypeStruct((batch, seq, D_MODEL), _F32),
        scratch_shapes=[pltpu.VMEM((n_lane_cols, ROWS + 2 * HALO_ROWS, LANES), _F32),
                        pltpu.VMEM((n_lane_cols, ROWS, LANES), _F32),
                        pltpu.VMEM((ROWS, D_MODEL), _BF16),
                        pltpu.VMEM((STEPS, SUBLANES, D_MODEL), _F32), pltpu.SemaphoreType.DMA((1,))],
        compiler_params=params,
        name="conv_mlp",
    )(c_glu, c_glu, c_glu, hn, merged_a, conv_taps, _row2(conv_b[0]), _row2(conv_ln_g[0]), _row2(conv_ln_b[0]),
      w_branch_b[0].astype(_BF16), w_g1, b_g1, w_out[0].astype(_BF16), _row2(b_out[0]),
      _row2(ln1_g[0]), _row2(ln1_b[0]), w_up[0].astype(_BF16), _row2(b_up[0]),
      w_down[0].astype(_BF16), _row2(b_down[0]), _row2(ln2_g[0]), _row2(ln2_b[0]))
```

```python
import functools

import jax
import jax.numpy as jnp
from jax.experimental import pallas as pl
from jax.experimental.pallas import tpu as pltpu

D_MODEL = 1024
D_RNN = 1536
RNN_BLOCKS = 16
RNN_BLOCK_W = D_RNN // RNN_BLOCKS
GATE_GROUP = 768
RNN_CONV_W = 4
RNN_CONV_LEFT = 2
RNN_CONV_RIGHT = RNN_CONV_W - 1 - RNN_CONV_LEFT
RG_C = 8.0
CONV_W = 31
CONV_HALF = (CONV_W - 1) // 2
D_FF = 4 * D_MODEL
LN_EPS = 1e-5
DEEPNORM_ALPHA = 2.0 ** 0.25

SUBLANES = 8
ROWS = 512
STEPS = ROWS // SUBLANES
SUB_ROWS = 128
HALF_ROWS = ROWS // 2
CONV_LN_ROWS = ROWS // 2
LANES = 128
MXU_COL = 256
CONV_ROWS = 64
HALO_ROWS = 128
VMEM_LIMIT = 56 * 1024 * 1024

_BF16 = jnp.bfloat16
_F32 = jnp.float32
_GELU_C = 0.7978845608028654
_F32_TINY = 1.1754943508222875e-38


def _ln(x, g, b):
    mu = jnp.mean(x, axis=-1, keepdims=True)
    xc = x - mu
    var = jnp.mean(xc * xc, axis=-1, keepdims=True)
    return xc * jax.lax.rsqrt(var + LN_EPS) * g + b


def _sigmoid(x):
    return 0.5 * jnp.tanh(0.5 * x) + 0.5


def _gelu_tanh(x):
    hx = 0.5 * x
    t = jnp.tanh(x * ((x * x) * (_GELU_C * 0.044715) + _GELU_C))
    return hx * t + hx


def _dot(a, w):
    return jnp.dot(a.astype(_BF16), w, preferred_element_type=_F32)


def _gate_logits(u, wg_refs):
    wlo_ref, wmid_ref, whi_ref = wg_refs
    ub = u.astype(_BF16)
    r_parts, i_parts = [], []
    for q in range(D_RNN // GATE_GROUP):
        k0 = q * GATE_GROUP
        lo = jnp.dot(ub[:, k0:k0 + 2 * MXU_COL], wlo_ref[q], preferred_element_type=_F32)
        mid = jnp.dot(ub[:, k0:k0 + 3 * MXU_COL], wmid_ref[q], preferred_element_type=_F32)
        hi = jnp.dot(ub[:, k0 + MXU_COL:k0 + 3 * MXU_COL], whi_ref[q], preferred_element_type=_F32)
        for part in (lo, mid, hi):
            r_parts.append(part[:, :MXU_COL])
            i_parts.append(part[:, MXU_COL:])
    return jnp.concatenate(r_parts, axis=1), jnp.concatenate(i_parts, axis=1)


def _gate_terms(u, logits, bg_ref, nhd_ref):
    t_r = jnp.tanh(logits[0] + bg_ref[0:1, :])
    t_i = jnp.tanh(logits[1] + bg_ref[1:2, :])
    nhd = nhd_ref[...]
    log_a = t_r * nhd + nhd
    a = jnp.exp(log_a)
    q2 = jnp.tanh(log_a) * (-1.0 - a * a)
    root = q2 * jax.lax.rsqrt(jnp.maximum(q2, _F32_TINY))
    return a, root * ((0.5 * t_i + 0.5) * u)


def _scan_block(a, bx, h, reverse):
    steps = a.shape[0] // SUBLANES
    outs = [None] * steps
    order = range(steps - 1, -1, -1) if reverse else range(steps)
    for s in order:
        sl = slice(s * SUBLANES, (s + 1) * SUBLANES)
        h = a[sl, :] * h + bx[sl, :]
        outs[s] = h
    return jnp.concatenate(outs, axis=0), h


def _x_tile_copies(x_hbm, xbuf, sems, n_tiles, tile, slot, start):
    t0 = tile * STEPS

    def run(src_start, n_steps, dst_start, sem):
        for b in range(SUBLANES):
            cp = pltpu.make_async_copy(x_hbm.at[b, pl.ds(src_start, n_steps), :],
                                       xbuf.at[slot, pl.ds(dst_start, n_steps), b, :], sem)
            if start:
                cp.start()
            else:
                cp.wait()

    run(t0, STEPS, RNN_CONV_LEFT, sems.at[slot, 0])

    @pl.when(tile > 0)
    def _():
        run(t0 - RNN_CONV_LEFT, RNN_CONV_LEFT, 0, sems.at[slot, 1])

    @pl.when(tile < n_tiles - 1)
    def _():
        run(t0 + STEPS, RNN_CONV_RIGHT, RNN_CONV_LEFT + STEPS, sems.at[slot, 2])


def _rnn_bwd_body(n_tiles, x_hbm, lng_ref, lnb_ref, wr_ref, br_ref, w4_ref, b4_ref,
                  wlo_ref, wmid_ref, whi_ref, bg_ref, decay_ref, wxc_ref, bxc_ref,
                  u_ref, hb_ref, hn_ref, c_ref, xbuf, sems, xr_scr, carry_scr):
    wg_ref = (wlo_ref, wmid_ref, whi_ref)
    i = pl.program_id(0)
    tile = n_tiles - 1 - i
    slot = i % 2
    lead = RNN_CONV_LEFT * SUBLANES
    ext_rows = (RNN_CONV_LEFT + STEPS + RNN_CONV_RIGHT) * SUBLANES

    @pl.when(i == 0)
    def _():
        carry_scr[...] = jnp.zeros_like(carry_scr)
        xbuf[...] = jnp.zeros_like(xbuf)
        _x_tile_copies(x_hbm, xbuf, sems, n_tiles, tile, slot, start=True)

    @pl.when(i + 1 < n_tiles)
    def _():
        _x_tile_copies(x_hbm, xbuf, sems, n_tiles, tile - 1, 1 - slot, start=True)

    _x_tile_copies(x_hbm, xbuf, sems, n_tiles, tile, slot, start=False)

    hn = _ln(xbuf[slot].reshape(ext_rows, D_MODEL), lng_ref[...], lnb_ref[...])
    hn_ref[...] = hn[lead:lead + ROWS, :]
    hnb = hn.astype(_BF16)

    xr = jnp.dot(hnb, wr_ref[...], preferred_element_type=_F32) + br_ref[...]
    xr_scr[0:lead, :] = jnp.where(tile > 0, xr[0:lead, :], 0.0)
    xr_scr[lead:lead + ROWS, :] = xr[lead:lead + ROWS, :]
    xr_scr[lead + ROWS:, :] = jnp.where(tile < n_tiles - 1, xr[lead + ROWS:, :], 0.0)

    def glu_piece(p):
        xc = jnp.dot(hnb[lead:lead + ROWS, :], wxc_ref[p], preferred_element_type=_F32) + bxc_ref[p]
        c = xc[:, :MXU_COL] * _sigmoid(xc[:, MXU_COL:])
        per = MXU_COL // LANES
        for j in range(per):
            c_ref[p * per + j] = c[:, j * LANES:(j + 1) * LANES]

    def conv_and_logits(sb):
        base = sb * SUB_ROWS
        u = jnp.broadcast_to(b4_ref[...], (SUB_ROWS, D_RNN))
        for k in range(RNN_CONV_W):
            u = u + w4_ref[k:k + 1, :] * xr_scr[base + k * SUBLANES:base + k * SUBLANES + SUB_ROWS, :]
        u_ref[base:base + SUB_ROWS, :] = u
        return u, _gate_logits(u, wg_ref)

    n_sub = ROWS // SUB_ROWS
    n_glu = wxc_ref.shape[0]
    h = carry_scr[...]
    ahead = conv_and_logits(n_sub - 1)
    for sb in range(n_sub - 1, -1, -1):
        base = sb * SUB_ROWS
        u, logits = ahead
        if sb > 0:
            ahead = conv_and_logits(sb - 1)
        step = n_sub - 1 - sb
        for p in range(step * n_glu // n_sub, (step + 1) * n_glu // n_sub):
            glu_piece(p)
        a, bx = _gate_terms(u, logits, bg_ref, decay_ref)
        hs, h = _scan_block(a, bx, h, reverse=True)
        hb_ref[base:base + SUB_ROWS, :] = hs
    carry_scr[...] = h


def _rnn_fwd_body(hn_ref, u_ref, hb_ref, wyg_ref, byg_ref, wlo_ref, wmid_ref, whi_ref, bg_ref, decay_ref,
                  wa_ref, ma_ref, z_scr, carry_scr):
    wg_ref = (wlo_ref, wmid_ref, whi_ref)

    @pl.when(pl.program_id(0) == 0)
    def _():
        carry_scr[...] = jnp.zeros_like(carry_scr)

    def logits_of(sb):
        u = u_ref[sb * SUB_ROWS:(sb + 1) * SUB_ROWS, :]
        return u, _gate_logits(u, wg_ref)

    n_sub = ROWS // SUB_ROWS
    ahead = logits_of(0)
    p = _dot(hn_ref[...], wyg_ref[...]) + byg_ref[...]
    h = carry_scr[...]
    for sb in range(n_sub):
        rows = slice(sb * SUB_ROWS, (sb + 1) * SUB_ROWS)
        u, logits = ahead
        if sb + 1 < n_sub:
            ahead = logits_of(sb + 1)
        a, bx = _gate_terms(u, logits, bg_ref, decay_ref)
        hs, h = _scan_block(a, bx, h, reverse=False)
        z_scr[rows, :] = (_gelu_tanh(p[rows, :D_RNN]) * (hs + hb_ref[rows, :])).astype(_BF16)
    carry_scr[...] = h
    y_a = jnp.dot(z_scr[...], wa_ref[...], preferred_element_type=_F32)
    ma_ref[...] = _sigmoid(p[:, D_RNN:]) * y_a


def _out_tile_copies(obuf, out_hbm, sem, tile, start):
    for b in range(SUBLANES):
        cp = pltpu.make_async_copy(obuf.at[:, b, :], out_hbm.at[b, pl.ds(tile * STEPS, STEPS), :], sem.at[0])
        if start:
            cp.start()
        else:
            cp.wait()


def _conv_mlp_body(n_tiles, cp_ref, cm_ref, cn_ref, hn_ref, ma_ref, cw_ref, cb_ref, clg_ref, clb_ref,
                   wb_ref, wg1_ref, bg1_ref, wout_ref, bout_ref, l1g_ref, l1b_ref,
                   wup_ref, bup_ref, wdn_ref, bdn_ref, l2g_ref, l2b_ref,
                   out_hbm, c_scr, conv_scr, cnb_scr, obuf, sem):
    s = pl.program_id(0)
    conv_tile = jnp.minimum(s, n_tiles - 1)
    n_lane_cols = D_MODEL // LANES

    @pl.when(s == 0)
    def _():
        cnb_scr[...] = jnp.zeros_like(cnb_scr)

    c_scr[:, 0:HALO_ROWS, :] = jnp.where(conv_tile > 0, cp_ref[...], 0.0)
    c_scr[:, HALO_ROWS:HALO_ROWS + ROWS, :] = cm_ref[...]
    c_scr[:, HALO_ROWS + ROWS:, :] = jnp.where(conv_tile < n_tiles - 1, cn_ref[...], 0.0)

    first = HALO_ROWS - CONV_HALF * SUBLANES
    groups = CONV_ROWS // SUBLANES
    never = s < 0
    n_conv_tiles = n_lane_cols * (ROWS // CONV_ROWS)
    tiles_per_half = n_conv_tiles // (ROWS // CONV_LN_ROWS)
    conv_state = {"prev": None, "next": 0, "credit": 0.0}
    pushes_per_step = (ROWS // 16) * (3 * (D_MODEL // MXU_COL) ** 2 + 2 * (D_MODEL // MXU_COL) * (D_FF // MXU_COL))

    def conv_tiles(count):
        for t in range(conv_state["next"], min(conv_state["next"] + count, n_conv_tiles)):
            r0, l = (t // n_lane_cols) * CONV_ROWS, t % n_lane_cols
            prev = conv_state["prev"]
            bias = jnp.broadcast_to(cb_ref[:, l * LANES:(l + 1) * LANES], (SUBLANES, LANES))
            acc = [bias if prev is None else jnp.where(never, prev[g], bias) for g in range(groups)]
            for k in range(CONV_W):
                w = cw_ref[l, k * SUBLANES:(k + 1) * SUBLANES, :]
                for g in range(groups):
                    row = r0 + first + (k + g) * SUBLANES
                    acc[g] = acc[g] + w * c_scr[l, row:row + SUBLANES, :]
            for g in range(groups):
                conv_scr[l, r0 + g * SUBLANES:r0 + (g + 1) * SUBLANES, :] = acc[g]
            conv_state["prev"] = acc
            if (t + 1) % tiles_per_half == 0:
                rows = slice((t + 1 - tiles_per_half) // n_lane_cols * CONV_ROWS, (t + 1) // n_lane_cols * CONV_ROWS)
                conv = jnp.concatenate([conv_scr[j, rows, :] for j in range(n_lane_cols)], axis=1)
                cn = _ln(conv, clg_ref[...], clb_ref[...])
                cnb_scr[rows, :] = (cn * _sigmoid(cn)).astype(_BF16)
        conv_state["next"] = min(conv_state["next"] + count, n_conv_tiles)

    def pieces(lhs, w_ref):
        k_tiles = w_ref.shape[0] // MXU_COL
        out = []
        for j in range(w_ref.shape[1] // MXU_COL):
            piece = jnp.dot(lhs, w_ref[:, j * MXU_COL:(j + 1) * MXU_COL], preferred_element_type=_F32)
            conv_state["credit"] += (lhs.shape[0] // 16) * k_tiles * n_conv_tiles / pushes_per_step
            whole = int(conv_state["credit"])
            conv_state["credit"] -= whole
            conv_tiles(whole)
            if whole and conv_state["prev"] is not None:
                corner = jnp.where(never, conv_state["prev"][0], piece[0:SUBLANES, 0:LANES])
                top = jnp.concatenate([corner, piece[0:SUBLANES, LANES:]], axis=1)
                piece = jnp.concatenate([top, piece[SUBLANES:, :]], axis=0)
            out.append(piece)
        return jnp.concatenate(out, axis=1)

    for r0 in range(0, ROWS, HALF_ROWS):
        rows = slice(r0, r0 + HALF_ROWS)
        hn = hn_ref[rows, :]
        y_b = pieces(cnb_scr[rows, :], wb_ref)
        g1 = _sigmoid(pieces(hn.astype(_BF16), wg1_ref) + bg1_ref[...])
        mixed = pieces((ma_ref[rows, :] + g1 * y_b).astype(_BF16), wout_ref) + bout_ref[...]
        h1 = _ln(DEEPNORM_ALPHA * hn + mixed, l1g_ref[...], l1b_ref[...])
        m = jnp.maximum(pieces(h1.astype(_BF16), wup_ref) + bup_ref[...], 0.0)
        y = pieces((m * m).astype(_BF16), wdn_ref) + bdn_ref[...]
        res = _ln(DEEPNORM_ALPHA * h1 + y, l2g_ref[...], l2b_ref[...])
        if r0 == 0:
            @pl.when(s >= 2)
            def _():
                _out_tile_copies(obuf, out_hbm, sem, s - 2, start=False)
        obuf[r0 // SUBLANES:(r0 + HALF_ROWS) // SUBLANES] = res.reshape(HALF_ROWS // SUBLANES, SUBLANES, D_MODEL)
    conv_tiles(n_conv_tiles)
    assert conv_state["next"] == n_conv_tiles

    @pl.when(s >= 1)
    def _():
        _out_tile_copies(obuf, out_hbm, sem, s - 1, start=True)

    @pl.when(s == n_tiles)
    def _():
        _out_tile_copies(obuf, out_hbm, sem, s - 1, start=False)


def _full(shape):
    return pl.BlockSpec(shape, lambda i: (0,) * len(shape), pipeline_mode=pl.Buffered(1))


def _row2(v):
    return v.reshape(1, -1).astype(_F32)


def _gate_weights(gate_w, gate_b):
    t = MXU_COL
    per_group = GATE_GROUP // RNN_BLOCK_W
    assert GATE_GROUP == 3 * t and per_group * RNN_BLOCK_W == GATE_GROUP
    assert (-(-t // RNN_BLOCK_W)) * RNN_BLOCK_W <= 2 * t
    eye = jnp.eye(per_group, dtype=gate_w.dtype)
    lo, mid, hi = [], [], []
    for q in range(D_RNN // GATE_GROUP):
        blk = gate_w[:, q * per_group:(q + 1) * per_group]
        dense = jnp.einsum("gnio,nm->gnimo", blk, eye).reshape(2, GATE_GROUP, GATE_GROUP)
        lo.append(jnp.concatenate([dense[0, :2 * t, :t], dense[1, :2 * t, :t]], axis=1))
        mid.append(jnp.concatenate([dense[0, :, t:2 * t], dense[1, :, t:2 * t]], axis=1))
        hi.append(jnp.concatenate([dense[0, t:, 2 * t:], dense[1, t:, 2 * t:]], axis=1))
    stacks = tuple((0.5 * jnp.stack(w)).astype(_BF16) for w in (lo, mid, hi))
    return stacks, 0.5 * gate_b.reshape(2, D_RNN).astype(_F32)


def kernel(x, emb_ln_g, emb_ln_b, w_in, b_in, rnn_conv_w, rnn_conv_b, rg_gate_w, rg_gate_b, rg_a_param, w_branch_a, conv_w, conv_b, conv_ln_g, conv_ln_b, w_branch_b, w_out, b_out, ln1_g, ln1_b, w_up, b_up, w_down, b_down, ln2_g, ln2_b):
    batch, seq, _ = x.shape
    assert batch == SUBLANES and x.shape[2] == D_MODEL and w_in.shape[0] == 1
    n_rows = batch * seq
    assert n_rows % ROWS == 0 and ROWS % SUB_ROWS == 0 and ROWS % HALO_ROWS == 0
    n_tiles = n_rows // ROWS
    ext_steps = RNN_CONV_LEFT + STEPS + RNN_CONV_RIGHT

    lng, lnb = _row2(emb_ln_g), _row2(emb_ln_b)
    w_in0, b_in0 = w_in[0], b_in[0]
    o_yr, o_xc, o_gl = D_RNN, 2 * D_RNN, 2 * D_RNN + 2 * D_MODEL
    w_r = w_in0[:, :o_yr].astype(_BF16)
    b_r = _row2(b_in0[:o_yr])
    w_yg = jnp.concatenate([w_in0[:, o_yr:o_xc], w_in0[:, o_gl:o_gl + D_MODEL]], axis=1).astype(_BF16)
    b_yg = _row2(jnp.concatenate([b_in0[o_yr:o_xc], b_in0[o_gl:o_gl + D_MODEL]]))
    w_xc = w_in0[:, o_xc:o_gl].astype(_BF16)
    b_xc = _row2(b_in0[o_xc:o_gl])
    w_g1 = w_in0[:, o_gl + D_MODEL:].astype(_BF16)
    b_g1 = _row2(b_in0[o_gl + D_MODEL:])
    wg_f, bg_f = _gate_weights(rg_gate_w[0, 0], rg_gate_b[0, 0])
    wg_b, bg_b = _gate_weights(rg_gate_w[0, 1], rg_gate_b[0, 1])
    decay = -0.5 * RG_C * jax.nn.softplus(-rg_a_param[0].astype(_F32))
    params = pltpu.CompilerParams(dimension_semantics=("arbitrary",), vmem_limit_bytes=VMEM_LIMIT)

    rev = lambda i: n_tiles - 1 - i
    n_lane_cols = D_MODEL // LANES
    n_glu = D_MODEL // MXU_COL
    pair = lambda a, p: jnp.concatenate([a[..., p * MXU_COL:(p + 1) * MXU_COL],
                                         a[..., D_MODEL + p * MXU_COL:D_MODEL + (p + 1) * MXU_COL]], axis=-1)
    w_glu = jnp.stack([pair(w_xc, p) for p in range(n_glu)])
    b_glu = jnp.stack([pair(b_xc, p) for p in range(n_glu)])
    u, h_bwd, hn, c_glu = pl.pallas_call(
        functools.partial(_rnn_bwd_body, n_tiles),
        grid=(n_tiles,),
        in_specs=[
            pl.BlockSpec(memory_space=pl.ANY),
            _full((1, D_MODEL)), _full((1, D_MODEL)),
            _full((D_MODEL, D_RNN)), _full((1, D_RNN)),
            _full((RNN_CONV_W, D_RNN)), _full((1, D_RNN)),
            *[_full(w.shape) for w in wg_b], _full((2, D_RNN)), _full((1, D_RNN)),
            _full(w_glu.shape), _full(b_glu.shape),
        ],
        out_specs=[pl.BlockSpec((ROWS, D_RNN), lambda i: (rev(i), 0)),
                   pl.BlockSpec((ROWS, D_RNN), lambda i: (rev(i), 0)),
                   pl.BlockSpec((ROWS, D_MODEL), lambda i: (rev(i), 0)),
                   pl.BlockSpec((n_lane_cols, ROWS, LANES), lambda i: (0, rev(i), 0))],
        out_shape=[jax.ShapeDtypeStruct((n_rows, D_RNN), _F32), jax.ShapeDtypeStruct((n_rows, D_RNN), _F32),
                   jax.ShapeDtypeStruct((n_rows, D_MODEL), _F32),
                   jax.ShapeDtypeStruct((n_lane_cols, n_rows, LANES), _F32)],
        scratch_shapes=[pltpu.VMEM((2, ext_steps, SUBLANES, D_MODEL), _F32), pltpu.SemaphoreType.DMA((2, 3)),
                        pltpu.VMEM((ext_steps * SUBLANES, D_RNN), _F32), pltpu.VMEM((SUBLANES, D_RNN), _F32)],
        compiler_params=params,
        name="rnn_bwd",
    )(x, lng, lnb, w_r, b_r, rnn_conv_w[0].astype(_F32), _row2(rnn_conv_b[0]), *wg_b, bg_b, decay[1:2],
      w_glu, b_glu)

    tile_spec = lambda d: pl.BlockSpec((ROWS, d), lambda i: (i, 0))
    merged_a = pl.pallas_call(
        _rnn_fwd_body,
        grid=(n_tiles,),
        in_specs=[
            tile_spec(D_MODEL), tile_spec(D_RNN), tile_spec(D_RNN),
            _full((D_MODEL, D_RNN + D_MODEL)), _full((1, D_RNN + D_MODEL)),
            *[_full(w.shape) for w in wg_f], _full((2, D_RNN)), _full((1, D_RNN)),
            _full((D_RNN, D_MODEL)),
        ],
        out_specs=tile_spec(D_MODEL),
        out_shape=jax.ShapeDtypeStruct((n_rows, D_MODEL), _F32),
        scratch_shapes=[pltpu.VMEM((ROWS, D_RNN), _BF16), pltpu.VMEM((SUBLANES, D_RNN), _F32)],
        compiler_params=params,
        name="rnn_fwd",
    )(hn, u, h_bwd, w_yg, b_yg, *wg_f, bg_f, decay[0:1], w_branch_a[0].astype(_BF16))

    halo_per_tile = ROWS // HALO_ROWS
    n_halo_blocks = n_rows // HALO_ROWS
    conv_tile = lambda s: jnp.minimum(s, n_tiles - 1)
    prev_tile_spec = pl.BlockSpec((ROWS, D_MODEL), lambda s: (jnp.maximum(s - 1, 0), 0))
    conv_taps = jnp.transpose(
        jnp.repeat(conv_w[0].astype(_F32), SUBLANES, axis=0).reshape(CONV_W * SUBLANES, n_lane_cols, LANES), (1, 0, 2))
    return pl.pallas_call(
        functools.partial(_conv_mlp_body, n_tiles),
        grid=(n_tiles + 1,),
        in_specs=[
            pl.BlockSpec((n_lane_cols, HALO_ROWS, LANES),
                         lambda s: (0, jnp.maximum(conv_tile(s) * halo_per_tile - 1, 0), 0)),
            pl.BlockSpec((n_lane_cols, ROWS, LANES), lambda s: (0, conv_tile(s), 0)),
            pl.BlockSpec((n_lane_cols, HALO_ROWS, LANES),
                         lambda s: (0, jnp.minimum((conv_tile(s) + 1) * halo_per_tile, n_halo_blocks - 1), 0)),
            prev_tile_spec, prev_tile_spec,
            _full((n_lane_cols, CONV_W * SUBLANES, LANES)), _full((1, D_MODEL)),
            _full((1, D_MODEL)), _full((1, D_MODEL)),
            _full((D_MODEL, D_MODEL)),
            _full((D_MODEL, D_MODEL)), _full((1, D_MODEL)),
            _full((D_MODEL, D_MODEL)), _full((1, D_MODEL)),
            _full((1, D_MODEL)), _full((1, D_MODEL)),
            _full((D_MODEL, D_FF)), _full((1, D_FF)),
            _full((D_FF, D_MODEL)), _full((1, D_MODEL)),
            _full((1, D_MODEL)), _full((1, D_MODEL)),
        ],
        out_specs=pl.BlockSpec(memory_space=pl.ANY),
        out_shape=jax.ShapeDtypeStruct((batch, seq, D_MODEL), _F32),
        scratch_shapes=[pltpu.VMEM((n_lane_cols, ROWS + 2 * HALO_ROWS, LANES), _F32),
                        pltpu.VMEM((n_lane_cols, ROWS, LANES), _F32),
                        pltpu.VMEM((ROWS, D_MODEL), _BF16),
                        pltpu.VMEM((STEPS, SUBLANES, D_MODEL), _F32), pltpu.SemaphoreType.DMA((1,))],
        compiler_params=params,
        name="conv_mlp",
    )(c_glu, c_glu, c_glu, hn, merged_a, conv_taps, _row2(conv_b[0]), _row2(conv_ln_g[0]), _row2(conv_ln_b[0]),
      w_branch_b[0].astype(_BF16), w_g1, b_g1, w_out[0].astype(_BF16), _row2(b_out[0]),
      _row2(ln1_g[0]), _row2(ln1_b[0]), w_up[0].astype(_BF16), _row2(b_up[0]),
      w_down[0].astype(_BF16), _row2(b_down[0]), _row2(ln2_g[0]), _row2(ln2_b[0]))
```

```python
import functools

import jax
import jax.numpy as jnp
from jax.experimental import pallas as pl
from jax.experimental.pallas import tpu as pltpu

D_MODEL = 1024
D_RNN = 1536
RNN_BLOCKS = 16
RNN_BLOCK_W = D_RNN // RNN_BLOCKS
GATE_GROUP = 768
RNN_CONV_W = 4
RNN_CONV_LEFT = 2
RNN_CONV_RIGHT = RNN_CONV_W - 1 - RNN_CONV_LEFT
RG_C = 8.0
CONV_W = 31
CONV_HALF = (CONV_W - 1) // 2
D_FF = 4 * D_MODEL
LN_EPS = 1e-5
DEEPNORM_ALPHA = 2.0 ** 0.25

SUBLANES = 8
ROWS = 512
STEPS = ROWS // SUBLANES
SUB_ROWS = 128
HALF_ROWS = ROWS // 2
CONV_LN_ROWS = ROWS // 2
LANES = 128
DMA_PRIORITIES = 2
MXU_COL = 256
CONV_ROWS = 64
HALO_ROWS = 128
VMEM_LIMIT = 56 * 1024 * 1024

_BF16 = jnp.bfloat16
_F32 = jnp.float32
_GELU_C = 0.7978845608028654
_F32_TINY = 1.1754943508222875e-38


def _ln(x, g, b):
    mu = jnp.mean(x, axis=-1, keepdims=True)
    xc = x - mu
    var = jnp.mean(xc * xc, axis=-1, keepdims=True)
    return xc * jax.lax.rsqrt(var + LN_EPS) * g + b


def _sigmoid(x):
    return 0.5 * jnp.tanh(0.5 * x) + 0.5


def _gelu_tanh(x):
    hx = 0.5 * x
    t = jnp.tanh(x * ((x * x) * (_GELU_C * 0.044715) + _GELU_C))
    return hx * t + hx


def _dot(a, w):
    return jnp.dot(a.astype(_BF16), w, preferred_element_type=_F32)


def _gate_logits(u, wg_refs):
    wlo_ref, wmid_ref, whi_ref = wg_refs
    ub = u.astype(_BF16)
    r_parts, i_parts = [], []
    for q in range(D_RNN // GATE_GROUP):
        k0 = q * GATE_GROUP
        lo = jnp.dot(ub[:, k0:k0 + 2 * MXU_COL], wlo_ref[q], preferred_element_type=_F32)
        mid = jnp.dot(ub[:, k0:k0 + 3 * MXU_COL], wmid_ref[q], preferred_element_type=_F32)
        hi = jnp.dot(ub[:, k0 + MXU_COL:k0 + 3 * MXU_COL], whi_ref[q], preferred_element_type=_F32)
        for part in (lo, mid, hi):
            r_parts.append(part[:, :MXU_COL])
            i_parts.append(part[:, MXU_COL:])
    return jnp.concatenate(r_parts, axis=1), jnp.concatenate(i_parts, axis=1)


def _gate_terms(u, logits, bg_ref, nhd_ref):
    t_r = jnp.tanh(logits[0] + bg_ref[0:1, :])
    t_i = jnp.tanh(logits[1] + bg_ref[1:2, :])
    nhd = nhd_ref[...]
    log_a = t_r * nhd + nhd
    a = jnp.exp(log_a)
    q2 = jnp.tanh(log_a) * (-1.0 - a * a)
    root = q2 * jax.lax.rsqrt(jnp.maximum(q2, _F32_TINY))
    return a, root * ((0.5 * t_i + 0.5) * u)


def _scan_block(a, bx, h, reverse):
    steps = a.shape[0] // SUBLANES
    outs = [None] * steps
    order = range(steps - 1, -1, -1) if reverse else range(steps)
    for s in order:
        sl = slice(s * SUBLANES, (s + 1) * SUBLANES)
        h = a[sl, :] * h + bx[sl, :]
        outs[s] = h
    return jnp.concatenate(outs, axis=0), h


def _x_tile_copies(x_hbm, xbuf, sems, n_tiles, tile, slot, start):
    t0 = tile * STEPS

    def run(src_start, n_steps, dst_start, sem):
        for b in range(SUBLANES):
            cp = pltpu.make_async_copy(x_hbm.at[b, pl.ds(src_start, n_steps), :],
                                       xbuf.at[slot, pl.ds(dst_start, n_steps), b, :], sem)
            if start:
                cp.start(priority=b % DMA_PRIORITIES)
            else:
                cp.wait()

    run(t0, STEPS, RNN_CONV_LEFT, sems.at[slot, 0])

    @pl.when(tile > 0)
    def _():
        run(t0 - RNN_CONV_LEFT, RNN_CONV_LEFT, 0, sems.at[slot, 1])

    @pl.when(tile < n_tiles - 1)
    def _():
        run(t0 + STEPS, RNN_CONV_RIGHT, RNN_CONV_LEFT + STEPS, sems.at[slot, 2])


def _rnn_bwd_body(n_tiles, x_hbm, lng_ref, lnb_ref, wr_ref, br_ref, w4_ref, b4_ref,
                  wlo_ref, wmid_ref, whi_ref, bg_ref, decay_ref, wxc_ref, bxc_ref,
                  u_ref, hb_ref, hn_ref, c_ref, xbuf, sems, xr_scr, carry_scr):
    wg_ref = (wlo_ref, wmid_ref, whi_ref)
    i = pl.program_id(0)
    tile = n_tiles - 1 - i
    slot = i % 2
    lead = RNN_CONV_LEFT * SUBLANES
    ext_rows = (RNN_CONV_LEFT + STEPS + RNN_CONV_RIGHT) * SUBLANES

    @pl.when(i == 0)
    def _():
        carry_scr[...] = jnp.zeros_like(carry_scr)
        xbuf[...] = jnp.zeros_like(xbuf)
        _x_tile_copies(x_hbm, xbuf, sems, n_tiles, tile, slot, start=True)

    @pl.when(i + 1 < n_tiles)
    def _():
        _x_tile_copies(x_hbm, xbuf, sems, n_tiles, tile - 1, 1 - slot, start=True)

    _x_tile_copies(x_hbm, xbuf, sems, n_tiles, tile, slot, start=False)

    hn = _ln(xbuf[slot].reshape(ext_rows, D_MODEL), lng_ref[...], lnb_ref[...])
    hn_ref[...] = hn[lead:lead + ROWS, :]
    hnb = hn.astype(_BF16)

    xr = jnp.dot(hnb, wr_ref[...], preferred_element_type=_F32) + br_ref[...]
    xr_scr[0:lead, :] = jnp.where(tile > 0, xr[0:lead, :], 0.0)
    xr_scr[lead:lead + ROWS, :] = xr[lead:lead + ROWS, :]
    xr_scr[lead + ROWS:, :] = jnp.where(tile < n_tiles - 1, xr[lead + ROWS:, :], 0.0)

    def glu_piece(p):
        xc = jnp.dot(hnb[lead:lead + ROWS, :], wxc_ref[p], preferred_element_type=_F32) + bxc_ref[p]
        c = xc[:, :MXU_COL] * _sigmoid(xc[:, MXU_COL:])
        per = MXU_COL // LANES
        for j in range(per):
            c_ref[p * per + j] = c[:, j * LANES:(j + 1) * LANES]

    def conv_and_logits(sb):
        base = sb * SUB_ROWS
        u = jnp.broadcast_to(b4_ref[...], (SUB_ROWS, D_RNN))
        for k in range(RNN_CONV_W):
            u = u + w4_ref[k:k + 1, :] * xr_scr[base + k * SUBLANES:base + k * SUBLANES + SUB_ROWS, :]
        u_ref[base:base + SUB_ROWS, :] = u
        return u, _gate_logits(u, wg_ref)

    n_sub = ROWS // SUB_ROWS
    n_glu = wxc_ref.shape[0]
    h = carry_scr[...]
    ahead = conv_and_logits(n_sub - 1)
    for sb in range(n_sub - 1, -1, -1):
        base = sb * SUB_ROWS
        u, logits = ahead
        if sb > 0:
            ahead = conv_and_logits(sb - 1)
        step = n_sub - 1 - sb
        for p in range(step * n_glu // n_sub, (step + 1) * n_glu // n_sub):
            glu_piece(p)
        a, bx = _gate_terms(u, logits, bg_ref, decay_ref)
        hs, h = _scan_block(a, bx, h, reverse=True)
        hb_ref[base:base + SUB_ROWS, :] = hs
    carry_scr[...] = h


def _rnn_fwd_body(hn_ref, u_ref, hb_ref, wyg_ref, byg_ref, wlo_ref, wmid_ref, whi_ref, bg_ref, decay_ref,
                  wa_ref, ma_ref, z_scr, carry_scr):
    wg_ref = (wlo_ref, wmid_ref, whi_ref)

    @pl.when(pl.program_id(0) == 0)
    def _():
        carry_scr[...] = jnp.zeros_like(carry_scr)

    def logits_of(sb):
        u = u_ref[sb * SUB_ROWS:(sb + 1) * SUB_ROWS, :]
        return u, _gate_logits(u, wg_ref)

    n_sub = ROWS // SUB_ROWS
    ahead = logits_of(0)
    p = _dot(hn_ref[...], wyg_ref[...]) + byg_ref[...]
    h = carry_scr[...]
    for sb in range(n_sub):
        rows = slice(sb * SUB_ROWS, (sb + 1) * SUB_ROWS)
        u, logits = ahead
        if sb + 1 < n_sub:
            ahead = logits_of(sb + 1)
        a, bx = _gate_terms(u, logits, bg_ref, decay_ref)
        hs, h = _scan_block(a, bx, h, reverse=False)
        z_scr[rows, :] = (_gelu_tanh(p[rows, :D_RNN]) * (hs + hb_ref[rows, :])).astype(_BF16)
    carry_scr[...] = h
    y_a = jnp.dot(z_scr[...], wa_ref[...], preferred_element_type=_F32)
    ma_ref[...] = _sigmoid(p[:, D_RNN:]) * y_a


def _out_tile_copies(obuf, out_hbm, sem, tile, start):
    for b in range(SUBLANES):
        cp = pltpu.make_async_copy(obuf.at[:, b, :], out_hbm.at[b, pl.ds(tile * STEPS, STEPS), :], sem.at[0])
        if start:
            cp.start(priority=b % DMA_PRIORITIES)
        else:
            cp.wait()


def _conv_mlp_body(n_tiles, cp_ref, cm_ref, cn_ref, hn_ref, ma_ref, cw_ref, cb_ref, clg_ref, clb_ref,
                   wb_ref, wg1_ref, bg1_ref, wout_ref, bout_ref, l1g_ref, l1b_ref,
                   wup_ref, bup_ref, wdn_ref, bdn_ref, l2g_ref, l2b_ref,
                   out_hbm, c_scr, conv_scr, cnb_scr, obuf, sem):
    s = pl.program_id(0)
    conv_tile = jnp.minimum(s, n_tiles - 1)
    n_lane_cols = D_MODEL // LANES

    @pl.when(s == 0)
    def _():
        cnb_scr[...] = jnp.zeros_like(cnb_scr)

    c_scr[:, 0:HALO_ROWS, :] = jnp.where(conv_tile > 0, cp_ref[...], 0.0)
    c_scr[:, HALO_ROWS:HALO_ROWS + ROWS, :] = cm_ref[...]
    c_scr[:, HALO_ROWS + ROWS:, :] = jnp.where(conv_tile < n_tiles - 1, cn_ref[...], 0.0)

    first = HALO_ROWS - CONV_HALF * SUBLANES
    groups = CONV_ROWS // SUBLANES
    never = s < 0
    n_conv_tiles = n_lane_cols * (ROWS // CONV_ROWS)
    tiles_per_half = n_conv_tiles // (ROWS // CONV_LN_ROWS)
    conv_state = {"prev": None, "next": 0, "credit": 0.0}
    pushes_per_step = (ROWS // 16) * (3 * (D_MODEL // MXU_COL) ** 2 + 2 * (D_MODEL // MXU_COL) * (D_FF // MXU_COL))

    def conv_tiles(count):
        for t in range(conv_state["next"], min(conv_state["next"] + count, n_conv_tiles)):
            r0, l = (t // n_lane_cols) * CONV_ROWS, t % n_lane_cols
            prev = conv_state["prev"]
            bias = jnp.broadcast_to(cb_ref[:, l * LANES:(l + 1) * LANES], (SUBLANES, LANES))
            acc = [bias if prev is None else jnp.where(never, prev[g], bias) for g in range(groups)]
            for k in range(CONV_W):
                w = cw_ref[l, k * SUBLANES:(k + 1) * SUBLANES, :]
                for g in range(groups):
                    row = r0 + first + (k + g) * SUBLANES
                    acc[g] = acc[g] + w * c_scr[l, row:row + SUBLANES, :]
            for g in range(groups):
                conv_scr[l, r0 + g * SUBLANES:r0 + (g + 1) * SUBLANES, :] = acc[g]
            conv_state["prev"] = acc
            if (t + 1) % tiles_per_half == 0:
                rows = slice((t + 1 - tiles_per_half) // n_lane_cols * CONV_ROWS, (t + 1) // n_lane_cols * CONV_ROWS)
                conv = jnp.concatenate([conv_scr[j, rows, :] for j in range(n_lane_cols)], axis=1)
                cn = _ln(conv, clg_ref[...], clb_ref[...])
                cnb_scr[rows, :] = (cn * _sigmoid(cn)).astype(_BF16)
        conv_state["next"] = min(conv_state["next"] + count, n_conv_tiles)

    def pieces(lhs, w_ref):
        k_tiles = w_ref.shape[0] // MXU_COL
        out = []
        for j in range(w_ref.shape[1] // MXU_COL):
            piece = jnp.dot(lhs, w_ref[:, j * MXU_COL:(j + 1) * MXU_COL], preferred_element_type=_F32)
            conv_state["credit"] += (lhs.shape[0] // 16) * k_tiles * n_conv_tiles / pushes_per_step
            whole = int(conv_state["credit"])
            conv_state["credit"] -= whole
            conv_tiles(whole)
            if whole and conv_state["prev"] is not None:
                corner = jnp.where(never, conv_state["prev"][0], piece[0:SUBLANES, 0:LANES])
                top = jnp.concatenate([corner, piece[0:SUBLANES, LANES:]], axis=1)
                piece = jnp.concatenate([top, piece[SUBLANES:, :]], axis=0)
            out.append(piece)
        return jnp.concatenate(out, axis=1)

    for r0 in range(0, ROWS, HALF_ROWS):
        rows = slice(r0, r0 + HALF_ROWS)
        hn = hn_ref[rows, :]
        y_b = pieces(cnb_scr[rows, :], wb_ref)
        g1 = _sigmoid(pieces(hn.astype(_BF16), wg1_ref) + bg1_ref[...])
        mixed = pieces((ma_ref[rows, :] + g1 * y_b).astype(_BF16), wout_ref) + bout_ref[...]
        h1 = _ln(DEEPNORM_ALPHA * hn + mixed, l1g_ref[...], l1b_ref[...])
        m = jnp.maximum(pieces(h1.astype(_BF16), wup_ref) + bup_ref[...], 0.0)
        y = pieces((m * m).astype(_BF16), wdn_ref) + bdn_ref[...]
        res = _ln(DEEPNORM_ALPHA * h1 + y, l2g_ref[...], l2b_ref[...])
        if r0 == 0:
            @pl.when(s >= 2)
            def _():
                _out_tile_copies(obuf, out_hbm, sem, s - 2, start=False)
        obuf[r0 // SUBLANES:(r0 + HALF_ROWS) // SUBLANES] = res.reshape(HALF_ROWS // SUBLANES, SUBLANES, D_MODEL)
    conv_tiles(n_conv_tiles)
    assert conv_state["next"] == n_conv_tiles

    @pl.when(s >= 1)
    def _():
        _out_tile_copies(obuf, out_hbm, sem, s - 1, start=True)

    @pl.when(s == n_tiles)
    def _():
        _out_tile_copies(obuf, out_hbm, sem, s - 1, start=False)


def _full(shape):
    return pl.BlockSpec(shape, lambda i: (0,) * len(shape), pipeline_mode=pl.Buffered(1))


def _row2(v):
    return v.reshape(1, -1).astype(_F32)


def _gate_weights(gate_w, gate_b):
    t = MXU_COL
    per_group = GATE_GROUP // RNN_BLOCK_W
    assert GATE_GROUP == 3 * t and per_group * RNN_BLOCK_W == GATE_GROUP
    assert (-(-t // RNN_BLOCK_W)) * RNN_BLOCK_W <= 2 * t
    eye = jnp.eye(per_group, dtype=gate_w.dtype)
    lo, mid, hi = [], [], []
    for q in range(D_RNN // GATE_GROUP):
        blk = gate_w[:, q * per_group:(q + 1) * per_group]
        dense = jnp.einsum("gnio,nm->gnimo", blk, eye).reshape(2, GATE_GROUP, GATE_GROUP)
        lo.append(jnp.concatenate([dense[0, :2 * t, :t], dense[1, :2 * t, :t]], axis=1))
        mid.append(jnp.concatenate([dense[0, :, t:2 * t], dense[1, :, t:2 * t]], axis=1))
        hi.append(jnp.concatenate([dense[0, t:, 2 * t:], dense[1, t:, 2 * t:]], axis=1))
    stacks = tuple((0.5 * jnp.stack(w)).astype(_BF16) for w in (lo, mid, hi))
    return stacks, 0.5 * gate_b.reshape(2, D_RNN).astype(_F32)


def kernel(x, emb_ln_g, emb_ln_b, w_in, b_in, rnn_conv_w, rnn_conv_b, rg_gate_w, rg_gate_b, rg_a_param, w_branch_a, conv_w, conv_b, conv_ln_g, conv_ln_b, w_branch_b, w_out, b_out, ln1_g, ln1_b, w_up, b_up, w_down, b_down, ln2_g, ln2_b):
    batch, seq, _ = x.shape
    assert batch == SUBLANES and x.shape[2] == D_MODEL and w_in.shape[0] == 1
    n_rows = batch * seq
    assert n_rows % ROWS == 0 and ROWS % SUB_ROWS == 0 and ROWS % HALO_ROWS == 0
    n_tiles = n_rows // ROWS
    ext_steps = RNN_CONV_LEFT + STEPS + RNN_CONV_RIGHT

    lng, lnb = _row2(emb_ln_g), _row2(emb_ln_b)
    w_in0, b_in0 = w_in[0], b_in[0]
    o_yr, o_xc, o_gl = D_RNN, 2 * D_RNN, 2 * D_RNN + 2 * D_MODEL
    w_r = w_in0[:, :o_yr].astype(_BF16)
    b_r = _row2(b_in0[:o_yr])
    w_yg = jnp.concatenate([w_in0[:, o_yr:o_xc], w_in0[:, o_gl:o_gl + D_MODEL]], axis=1).astype(_BF16)
    b_yg = _row2(jnp.concatenate([b_in0[o_yr:o_xc], b_in0[o_gl:o_gl + D_MODEL]]))
    w_xc = w_in0[:, o_xc:o_gl].astype(_BF16)
    b_xc = _row2(b_in0[o_xc:o_gl])
    w_g1 = w_in0[:, o_gl + D_MODEL:].astype(_BF16)
    b_g1 = _row2(b_in0[o_gl + D_MODEL:])
    wg_f, bg_f = _gate_weights(rg_gate_w[0, 0], rg_gate_b[0, 0])
    wg_b, bg_b = _gate_weights(rg_gate_w[0, 1], rg_gate_b[0, 1])
    decay = -0.5 * RG_C * jax.nn.softplus(-rg_a_param[0].astype(_F32))
    params = pltpu.CompilerParams(dimension_semantics=("arbitrary",), vmem_limit_bytes=VMEM_LIMIT)

    rev = lambda i: n_tiles - 1 - i
    n_lane_cols = D_MODEL // LANES
    n_glu = D_MODEL // MXU_COL
    pair = lambda a, p: jnp.concatenate([a[..., p * MXU_COL:(p + 1) * MXU_COL],
                                         a[..., D_MODEL + p * MXU_COL:D_MODEL + (p + 1) * MXU_COL]], axis=-1)
    w_glu = jnp.stack([pair(w_xc, p) for p in range(n_glu)])
    b_glu = jnp.stack([pair(b_xc, p) for p in range(n_glu)])
    u, h_bwd, hn, c_glu = pl.pallas_call(
        functools.partial(_rnn_bwd_body, n_tiles),
        grid=(n_tiles,),
        in_specs=[
            pl.BlockSpec(memory_space=pl.ANY),
            _full((1, D_MODEL)), _full((1, D_MODEL)),
            _full((D_MODEL, D_RNN)), _full((1, D_RNN)),
            _full((RNN_CONV_W, D_RNN)), _full((1, D_RNN)),
            *[_full(w.shape) for w in wg_b], _full((2, D_RNN)), _full((1, D_RNN)),
            _full(w_glu.shape), _full(b_glu.shape),
        ],
        out_specs=[pl.BlockSpec((ROWS, D_RNN), lambda i: (rev(i), 0)),
                   pl.BlockSpec((ROWS, D_RNN), lambda i: (rev(i), 0)),
                   pl.BlockSpec((ROWS, D_MODEL), lambda i: (rev(i), 0)),
                   pl.BlockSpec((n_lane_cols, ROWS, LANES), lambda i: (0, rev(i), 0))],
        out_shape=[jax.ShapeDtypeStruct((n_rows, D_RNN), _F32), jax.ShapeDtypeStruct((n_rows, D_RNN), _F32),
                   jax.ShapeDtypeStruct((n_rows, D_MODEL), _F32),
                   jax.ShapeDtypeStruct((n_lane_cols, n_rows, LANES), _F32)],
        scratch_shapes=[pltpu.VMEM((2, ext_steps, SUBLANES, D_MODEL), _F32), pltpu.SemaphoreType.DMA((2, 3)),
                        pltpu.VMEM((ext_steps * SUBLANES, D_RNN), _F32), pltpu.VMEM((SUBLANES, D_RNN), _F32)],
        compiler_params=params,
        name="rnn_bwd",
    )(x, lng, lnb, w_r, b_r, rnn_conv_w[0].astype(_F32), _row2(rnn_conv_b[0]), *wg_b, bg_b, decay[1:2],
      w_glu, b_glu)

    tile_spec = lambda d: pl.BlockSpec((ROWS, d), lambda i: (i, 0))
    merged_a = pl.pallas_call(
        _rnn_fwd_body,
        grid=(n_tiles,),
        in_specs=[
            tile_spec(D_MODEL), tile_spec(D_RNN), tile_spec(D_RNN),
            _full((D_MODEL, D_RNN + D_MODEL)), _full((1, D_RNN + D_MODEL)),
            *[_full(w.shape) for w in wg_f], _full((2, D_RNN)), _full((1, D_RNN)),
            _full((D_RNN, D_MODEL)),
        ],
        out_specs=tile_spec(D_MODEL),
        out_shape=jax.ShapeDtypeStruct((n_rows, D_MODEL), _F32),
        scratch_shapes=[pltpu.VMEM((ROWS, D_RNN), _BF16), pltpu.VMEM((SUBLANES, D_RNN), _F32)],
        compiler_params=params,
        name="rnn_fwd",
    )(hn, u, h_bwd, w_yg, b_yg, *wg_f, bg_f, decay[0:1], w_branch_a[0].astype(_BF16))

    halo_per_tile = ROWS // HALO_ROWS
    n_halo_blocks = n_rows // HALO_ROWS
    conv_tile = lambda s: jnp.minimum(s, n_tiles - 1)
    prev_tile_spec = pl.BlockSpec((ROWS, D_MODEL), lambda s: (jnp.maximum(s - 1, 0), 0))
    conv_taps = jnp.transpose(
        jnp.repeat(conv_w[0].astype(_F32), SUBLANES, axis=0).reshape(CONV_W * SUBLANES, n_lane_cols, LANES), (1, 0, 2))
    return pl.pallas_call(
        functools.partial(_conv_mlp_body, n_tiles),
        grid=(n_tiles + 1,),
        in_specs=[
            pl.BlockSpec((n_lane_cols, HALO_ROWS, LANES),
                         lambda s: (0, jnp.maximum(conv_tile(s) * halo_per_tile - 1, 0), 0)),
            pl.BlockSpec((n_lane_cols, ROWS, LANES), lambda s: (0, conv_tile(s), 0)),
            pl.BlockSpec((n_lane_cols, HALO_ROWS, LANES),
                         lambda s: (0, jnp.minimum((conv_tile(s) + 1) * halo_per_tile, n_halo_blocks - 1), 0)),
            prev_tile_spec, prev_tile_spec,
            _full((n_lane_cols, CONV_W * SUBLANES, LANES)), _full((1, D_MODEL)),
            _full((1, D_MODEL)), _full((1, D_MODEL)),
            _full((D_MODEL, D_MODEL)),
            _full((D_MODEL, D_MODEL)), _full((1, D_MODEL)),
            _full((D_MODEL, D_MODEL)), _full((1, D_MODEL)),
            _full((1, D_MODEL)), _full((1, D_MODEL)),
            _full((D_MODEL, D_FF)), _full((1, D_FF)),
            _full((D_FF, D_MODEL)), _full((1, D_MODEL)),
            _full((1, D_MODEL)), _full((1, D_MODEL)),
        ],
        out_specs=pl.BlockSpec(memory_space=pl.ANY),
        out_shape=jax.ShapeDtypeStruct((batch, seq, D_MODEL), _F32),
        scratch_shapes=[pltpu.VMEM((n_lane_cols, ROWS + 2 * HALO_ROWS, LANES), _F32),
                        pltpu.VMEM((n_lane_cols, ROWS, LANES), _F32),
                        pltpu.VMEM((ROWS, D_MODEL), _BF16),
                        pltpu.VMEM((STEPS, SUBLANES, D_MODEL), _F32), pltpu.SemaphoreType.DMA((1,))],
        compiler_params=params,
        name="conv_mlp",
    )(c_glu, c_glu, c_glu, hn, merged_a, conv_taps, _row2(conv_b[0]), _row2(conv_ln_g[0]), _row2(conv_ln_b[0]),
      w_branch_b[0].astype(_BF16), w_g1, b_g1, w_out[0].astype(_BF16), _row2(b_out[0]),
      _row2(ln1_g[0]), _row2(ln1_b[0]), w_up[0].astype(_BF16), _row2(b_up[0]),
      w_down[0].astype(_BF16), _row2(b_down[0]), _row2(ln2_g[0]), _row2(ln2_b[0]))
```
